```python
import jax, jax.numpy as jnp
from jax import lax
import numpy as np

D_MODEL = 1024
BATCH = 32
SEQ = 256
DEPTH = 4
DEC_BATCH = 4
DEC_SEQ = 2048
PAST_LEN = 512

GRID_W = 64
D_MIX = D_MODEL
D_A = D_MIX // 4
RG_BLOCKS = 4
RG_BLOCK_DIM = D_A // RG_BLOCKS
RG_C = 8.0
CONV_W = 4
CONV_PAD = (2, 1)
D_B = D_MIX // 4
HG_HEADS = 4
HG_DIM = D_B // HG_HEADS
HG_CHUNK = 16
D_C = D_MIX - D_A - D_B
NA_DIM = 64
NA_HEADS = D_C // NA_DIM
NA_SCALE = NA_DIM ** -0.5
WIN_R = 8
WIN_C = 16
Q_BLOCK = 128
D_FF = 4 * D_MODEL
SPLIT_SIZES = (D_A, D_A, D_B, D_B, D_B, D_B, D_B, D_C, D_C, D_C)
D_IN = 2 * D_A + 5 * D_B + 3 * D_C
EPS = 1e-6
F32 = jnp.float32

kernel_name = 'hybrid_rglru_hgrn2_natten_flow_step'


def rms_norm(x, g):
    xf = x.astype(F32)
    y = xf * lax.rsqrt(jnp.mean(xf * xf, axis=-1, keepdims=True) + EPS)
    return (y * g.astype(F32)).astype(x.dtype)


def ada_modulation(cond, w_mod, b_mod):
    m = (jax.nn.silu(cond) @ w_mod + b_mod)[..., None, :]
    return jnp.split(m, 6, axis=-1)


def centred_conv(x, w, b):
    y = lax.conv_general_dilated(x, w[:, None, :].astype(x.dtype), window_strides=(1,),
                                 padding=[CONV_PAD], dimension_numbers=('NWC', 'WIO', 'NWC'),
                                 feature_group_count=x.shape[-1])
    return y + b


def linear_scan(a, b, h0):
    def combine(left, right):
        al, bl = left
        ar, br = right
        return al * ar, ar * bl + br
    a_cum, b_cum = lax.associative_scan(combine, (a, b), axis=1)
    return b_cum + a_cum * h0[:, None, :]


def rglru_direction(xc, w_a, b_a, w_x, b_x, lam, h0):
    bsz, t_len, ch = xc.shape
    xb = xc.reshape(bsz, t_len, RG_BLOCKS, RG_BLOCK_DIM)
    r = jax.nn.sigmoid(jnp.einsum('btnd,nde->btne', xb, w_a).reshape(bsz, t_len, ch) + b_a)
    i = jax.nn.sigmoid(jnp.einsum('btnd,nde->btne', xb, w_x).reshape(bsz, t_len, ch) + b_x)
    log_a = -RG_C * r.astype(F32) * jax.nn.softplus(-lam.astype(F32))
    b = jnp.sqrt(-jnp.expm1(2.0 * log_a)) * (i * xc).astype(F32)
    return linear_scan(jnp.exp(log_a), b, h0)


def rglru_bidir(xc, w_a, b_a, w_x, b_x, lam, h0):
    h_f = rglru_direction(xc, w_a[0], b_a[0], w_x[0], b_x[0], lam[0], h0[:, 0])
    h_b = rglru_direction(xc[:, ::-1], w_a[1], b_a[1], w_x[1], b_x[1], lam[1], h0[:, 1])[:, ::-1]
    final = jnp.stack([h_f[:, -1], h_b[:, 0]], axis=1)
    return h_f + h_b, final


def hgrn2_chunkwise(q, k, v, logf, s0):
    bsz, t_len, nh, _ = q.shape
    n_chunks = t_len // HG_CHUNK

    def chunk(t):
        return t.reshape(bsz, n_chunks, HG_CHUNK, nh, t.shape[-1])

    q, k, v, logf = chunk(q), chunk(k), chunk(v), chunk(logf)
    cum = jnp.cumsum(logf, axis=2)
    tot = cum[:, :, -1]
    causal = jnp.tril(jnp.ones((HG_CHUNK, HG_CHUNK), dtype=bool))[None, None, :, :, None, None]
    diff = cum[:, :, :, None] - cum[:, :, None]
    decay = jnp.exp(jnp.where(causal, diff, -jnp.inf))
    scores = jnp.einsum('bnthk,bntshk,bnshk->bnhts', q, decay, k)
    o_intra = jnp.einsum('bnhts,bnshv->bnthv', scores, v)
    ds = jnp.einsum('bnchk,bnchv->bnhkv', k * jnp.exp(tot[:, :, None] - cum), v)

    def step(s, inp):
        dec, d = inp
        return dec[..., None] * s + d, s

    s_fin, s_in = lax.scan(step, s0, (jnp.moveaxis(jnp.exp(tot), 1, 0), jnp.moveaxis(ds, 1, 0)))
    s_in = jnp.moveaxis(s_in, 0, 1)
    o_inter = jnp.einsum('bnchk,bnhkv->bnchv', q * jnp.exp(cum), s_in)
    return (o_intra + o_inter).reshape(bsz, t_len, nh, v.shape[-1]), s_fin


def hgrn2_bidir(q, k_f, k_b, v, logf_f, logf_b, s0):
    o_f, s_f = hgrn2_chunkwise(q, k_f, v, logf_f, s0[:, 0])
    o_b, s_b = hgrn2_chunkwise(q[:, ::-1], k_b[:, ::-1], v[:, ::-1], logf_b[:, ::-1], s0[:, 1])
    return o_f + o_b[:, ::-1], jnp.stack([s_f, s_b], axis=1)


def dense_context_attention(q, k, v):
    bsz, l_len, nh, hd = q.shape
    qb = jnp.moveaxis(q.reshape(bsz, l_len // Q_BLOCK, Q_BLOCK, nh, hd), 1, 0)

    def block(qi):
        s = jnp.einsum('bqhd,bkhd->bhqk', qi, k).astype(F32) * NA_SCALE
        p = jax.nn.softmax(s, axis=-1).astype(v.dtype)
        return jnp.einsum('bhqk,bkhd->bqhd', p, v)

    o = lax.map(block, qb)
    return jnp.moveaxis(o, 0, 1).reshape(bsz, l_len, nh, hd)


def neighbourhood_attention(q, k, v, ctx_k, ctx_v, rpb):
    bsz, t_len, nh, hd = q.shape
    rows = t_len // GRID_W
    kr = min(WIN_R, rows)
    r = np.arange(rows)
    ridx = np.clip(r - kr // 2, 0, rows - kr)[:, None] + np.arange(kr)[None]
    col = np.arange(GRID_W)
    c0 = np.clip(col - WIN_C // 2, 0, GRID_W - WIN_C)
    col_mask = (col[None] >= c0[:, None]) & (col[None] < c0[:, None] + WIN_C)
    dy = ridx - r[:, None] + WIN_R - 1
    dx = np.clip(col[None] - col[:, None], 1 - WIN_C, WIN_C - 1) + WIN_C - 1
    bias = rpb.astype(F32)[:, dy[:, None, :, None], dx[None, :, None, :]]
    qg = q.reshape(bsz, rows, GRID_W, nh, hd)
    kg = k.reshape(bsz, rows, GRID_W, nh, hd)[:, ridx]
    vg = v.reshape(bsz, rows, GRID_W, nh, hd)[:, ridx]
    s_loc = jnp.einsum('brchd,bruwhd->bhrcuw', qg, kg).astype(F32) * NA_SCALE + bias[None]
    s_loc = jnp.where(jnp.asarray(col_mask)[None, None, None, :, None, :], s_loc, -jnp.inf)
    s_ctx = jnp.einsum('brchd,blhd->bhrcl', qg, ctx_k).astype(F32) * NA_SCALE
    n_loc = kr * GRID_W
    s_all = jnp.concatenate([s_loc.reshape(bsz, nh, rows, GRID_W, n_loc), s_ctx], axis=-1)
    probs = jax.nn.softmax(s_all, axis=-1).astype(v.dtype)
    p_loc = probs[..., :n_loc].reshape(bsz, nh, rows, GRID_W, kr, GRID_W)
    o = (jnp.einsum('bhrcuw,bruwhd->brchd', p_loc, vg)
         + jnp.einsum('bhrcl,blhd->brchd', probs[..., n_loc:], ctx_v))
    return o.reshape(bsz, t_len, nh, hd)


def trunk_layer(x, cond, p, rg_h0, hg_s0, ctx_k, ctx_v):
    is_context = ctx_k is None
    bsz, t_len, _ = x.shape
    sh1, sc1, g1, sh2, sc2, g2 = ada_modulation(cond, p['w_mod'], p['b_mod'])
    h = rms_norm(x, p['norm1']) * (1 + sc1) + sh1
    split_points = np.cumsum(SPLIT_SIZES)[:-1].tolist()
    xa, ga, bq, bff, bfb, bi, bg, cq, ck, cv = jnp.split(h @ p['w_in'], split_points, axis=-1)
    if is_context:
        rg_h0 = jnp.zeros((bsz, 2, D_A), F32)
        hg_s0 = jnp.zeros((bsz, 2, HG_HEADS, HG_DIM, HG_DIM), F32)

    xc = centred_conv(xa, p['rg_conv_w'], p['rg_conv_b'])
    ya, rg_fin = rglru_bidir(xc, p['rg_w_a'], p['rg_b_a'], p['rg_w_x'], p['rg_b_x'], p['rg_lambda'],
                             rg_h0.astype(F32))
    out_a = jax.nn.gelu(ga) * ya.astype(x.dtype)

    def heads_b(t):
        return t.reshape(bsz, t_len, HG_HEADS, HG_DIM)

    lb = p['hg_lb']
    f_f = lb[0] + (1 - lb[0]) * jax.nn.sigmoid(bff.astype(F32))
    f_b = lb[1] + (1 - lb[1]) * jax.nn.sigmoid(bfb.astype(F32))
    ob, hg_fin = hgrn2_bidir(heads_b(jax.nn.silu(bq).astype(F32)), heads_b(1 - f_f), heads_b(1 - f_b),
                             heads_b(bi.astype(F32)), heads_b(jnp.log(f_f)), heads_b(jnp.log(f_b)),
                             hg_s0.astype(F32))
    out_b = (rms_norm(ob, p['hg_norm'].reshape(HG_HEADS, HG_DIM)).reshape(bsz, t_len, D_B).astype(x.dtype)
             * jax.nn.silu(bg))

    def heads_c(t):
        return t.reshape(bsz, t_len, NA_HEADS, NA_DIM)

    q, k, v = heads_c(cq), heads_c(ck), heads_c(cv)
    if is_context:
        out_c = dense_context_attention(q, k, v)
    else:
        out_c = neighbourhood_attention(q, k, v, ctx_k, ctx_v, p['na_rpb'])

    mix = jnp.concatenate([out_a, out_b, out_c.reshape(bsz, t_len, D_C)], axis=-1) @ p['w_out']
    x = x + g1 * mix
    h2 = rms_norm(x, p['norm2']) * (1 + sc2) + sh2
    x = x + g2 * (jnp.square(jax.nn.relu(h2 @ p['w1'])) @ p['w2'])
    if is_context:
        return x, (k, v, rg_fin.astype(x.dtype), hg_fin.astype(x.dtype))
    return x, None


def setup_inputs(seed: int = 0) -> dict:
    key = jax.random.key(seed)
    ks = jax.random.split(key, 32)

    def nrm(k, shape, s):
        return jax.random.normal(k, shape, jnp.float32) * s

    x_prompt = nrm(ks[0], (BATCH, SEQ, D_MODEL), 1.0)
    x_sample = nrm(ks[1], (DEC_BATCH, DEC_SEQ, D_MODEL), 1.0)
    cache_k = nrm(ks[2], (DEC_BATCH, DEPTH, PAST_LEN, NA_HEADS, NA_DIM), 1.0)
    cache_v = nrm(ks[3], (DEC_BATCH, DEPTH, PAST_LEN, NA_HEADS, NA_DIM), 1.0)
    state_rglru = nrm(ks[4], (DEC_BATCH, DEPTH, 2, D_A), 1.0)
    state_hgrn = nrm(ks[5], (DEC_BATCH, DEPTH, 2, HG_HEADS, HG_DIM, HG_DIM), 0.5)
    c = nrm(ks[6], (DEC_BATCH, D_MODEL), 1.0)
    c_ctx = nrm(ks[7], (D_MODEL,), 1.0)
    w_mod = nrm(ks[8], (DEPTH, D_MODEL, 6 * D_MODEL), 0.5 * D_MODEL ** -0.5)
    b_mod = nrm(ks[9], (DEPTH, 6 * D_MODEL), 0.01)
    norm1 = 1.0 + nrm(ks[10], (DEPTH, D_MODEL), 0.02)
    norm2 = 1.0 + nrm(ks[11], (DEPTH, D_MODEL), 0.02)
    w_in = nrm(ks[12], (DEPTH, D_MODEL, D_IN), D_MODEL ** -0.5)
    rg_conv_w = nrm(ks[13], (DEPTH, CONV_W, D_A), CONV_W ** -0.5)
    rg_conv_b = nrm(ks[14], (DEPTH, D_A), 0.01)
    rg_w_a = nrm(ks[15], (DEPTH, 2, RG_BLOCKS, RG_BLOCK_DIM, RG_BLOCK_DIM), RG_BLOCK_DIM ** -0.5)
    rg_b_a = nrm(ks[16], (DEPTH, 2, D_A), 0.01)
    rg_w_x = nrm(ks[17], (DEPTH, 2, RG_BLOCKS, RG_BLOCK_DIM, RG_BLOCK_DIM), RG_BLOCK_DIM ** -0.5)
    rg_b_x = nrm(ks[18], (DEPTH, 2, D_A), 0.01)
    a0 = jax.random.uniform(ks[19], (DEPTH, 2, D_A), jnp.float32, minval=0.9, maxval=0.999)
    s = a0 ** (1.0 / RG_C)
    rg_lambda = jnp.log(s) - jnp.log1p(-s)
    hg_lb = nrm(ks[20], (DEPTH, 2, D_B), 1.0)
    hg_norm = 1.0 + nrm(ks[21], (DEPTH, D_B), 0.02)
    na_rpb = nrm(ks[22], (DEPTH, NA_HEADS, 2 * WIN_R - 1, 2 * WIN_C - 1), 0.1)
    w_out = nrm(ks[23], (DEPTH, D_MIX, D_MODEL), D_MIX ** -0.5)
    w1 = nrm(ks[24], (DEPTH, D_MODEL, D_FF), D_MODEL ** -0.5)
    w2 = nrm(ks[25], (DEPTH, D_FF, D_MODEL), D_FF ** -0.5)
    norm_f = 1.0 + nrm(ks[26], (D_MODEL,), 0.02)
    return {'x_prompt': x_prompt, 'x_sample': x_sample, 'cache_k': cache_k, 'cache_v': cache_v,
            'state_rglru': state_rglru, 'state_hgrn': state_hgrn, 'c': c, 'c_ctx': c_ctx,
            'w_mod': w_mod, 'b_mod': b_mod, 'norm1': norm1, 'norm2': norm2, 'w_in': w_in,
            'rg_conv_w': rg_conv_w, 'rg_conv_b': rg_conv_b, 'rg_w_a': rg_w_a, 'rg_b_a': rg_b_a,
            'rg_w_x': rg_w_x, 'rg_b_x': rg_b_x, 'rg_lambda': rg_lambda, 'hg_lb': hg_lb,
            'hg_norm': hg_norm, 'na_rpb': na_rpb, 'w_out': w_out, 'w1': w1, 'w2': w2, 'norm_f': norm_f}


def reference(x_prompt, x_sample, cache_k, cache_v, state_rglru, state_hgrn, c, c_ctx,
              w_mod, b_mod, norm1, norm2, w_in, rg_conv_w, rg_conv_b, rg_w_a, rg_b_a, rg_w_x, rg_b_x,
              rg_lambda, hg_lb, hg_norm, na_rpb, w_out, w1, w2, norm_f):
    lb_w = jax.nn.softmax(hg_lb.astype(F32), axis=0)
    hg_lower = jnp.cumsum(lb_w, axis=0) - lb_w[0]
    xp, xs = x_prompt, x_sample
    new_k, new_v, new_rg, new_hg = [], [], [], []
    for l in range(DEPTH):
        p = {'w_mod': w_mod[l], 'b_mod': b_mod[l], 'norm1': norm1[l], 'norm2': norm2[l],
             'w_in': w_in[l], 'rg_conv_w': rg_conv_w[l], 'rg_conv_b': rg_conv_b[l],
             'rg_w_a': rg_w_a[l], 'rg_b_a': rg_b_a[l], 'rg_w_x': rg_w_x[l], 'rg_b_x': rg_b_x[l],
             'rg_lambda': rg_lambda[l], 'hg_lb': hg_lower[l], 'hg_norm': hg_norm[l],
             'na_rpb': na_rpb[l], 'w_out': w_out[l], 'w1': w1[l], 'w2': w2[l]}
        xp, (k_l, v_l, rg_l, hg_l) = trunk_layer(xp, c_ctx, p, None, None, None, None)
        new_k.append(k_l)
        new_v.append(v_l)
        new_rg.append(rg_l)
        new_hg.append(hg_l)
        xs, _ = trunk_layer(xs, c, p, state_rglru[:, l], state_hgrn[:, l], cache_k[:, l], cache_v[:, l])
    y_prompt = rms_norm(xp, norm_f)
    y_sample = rms_norm(xs, norm_f)
    return (y_prompt, y_sample, jnp.stack(new_k, axis=1), jnp.stack(new_v, axis=1),
            jnp.stack(new_rg, axis=1), jnp.stack(new_hg, axis=1))
```

```python
import functools

import numpy as np
import jax
import jax.numpy as jnp
from jax import lax
from jax.experimental import pallas as pl
from jax.experimental.pallas import tpu as pltpu

F32 = jnp.float32
BF16 = jnp.bfloat16

D_MODEL = 1024
BATCH = 32
SEQ = 256
DEPTH = 4
DEC_BATCH = 4
DEC_SEQ = 2048
PAST_LEN = 512
GRID_W = 64
GRID_ROWS = DEC_SEQ // GRID_W
D_A = 256
RG_BLOCKS = 4
RG_BLOCK_DIM = 64
RG_C = 8.0
D_B = 256
HG_HEADS = 4
HG_DIM = 64
HG_CHUNK = 16
D_C = 512
NA_DIM = 64
NA_HEADS = 8
NA_SCALE = NA_DIM ** -0.5
WIN_R = 8
WIN_C = 16
D_FF = 4 * D_MODEL
D_AB = 2 * D_A + 5 * D_B
D_IN = D_AB + 3 * D_C
EPS = 1e-6

N_PROMPT = BATCH * SEQ
N_TOK = N_PROMPT + DEC_BATCH * DEC_SEQ
TM = 512
N_PROMPT_TM = N_PROMPT // TM
MIX_ROWS = 2048
N_MIX_STEPS = N_TOK // MIX_ROWS
N_PROMPT_MIX = N_PROMPT // MIX_ROWS
SEG = 256
N_SEG = MIX_ROWS // SEG
N_COND = 8
NEG_BIG = -1e30

VMEM_LIMIT = 56 * 1024 * 1024


def _cparams(n_axes):
    return pltpu.CompilerParams(dimension_semantics=("arbitrary",) * n_axes,
                                vmem_limit_bytes=VMEM_LIMIT)


def _cond_of_tm_block(i):
    return jnp.where(i < N_PROMPT_TM, 0, 1 + (i - N_PROMPT_TM) // (DEC_SEQ // TM))


def _rms(x):
    return x * lax.rsqrt(jnp.mean(x * x, axis=-1, keepdims=True) + EPS)


def _mod_kernel(cond_ref, w_ref, b_ref, o_ref):
    s = jax.nn.silu(cond_ref[...])
    o_ref[0] = jnp.dot(s.astype(BF16), w_ref[0].astype(BF16), preferred_element_type=F32) + b_ref[0]


def _modulation(cond, w_mod, b_mod):
    nb = 6
    out = pl.pallas_call(
        _mod_kernel,
        grid=(DEPTH, nb),
        in_specs=[pl.BlockSpec((N_COND, D_MODEL), lambda l, j: (0, 0)),
                  pl.BlockSpec((1, D_MODEL, D_MODEL), lambda l, j: (l, 0, j)),
                  pl.BlockSpec((1, 1, D_MODEL), lambda l, j: (l, 0, j))],
        out_specs=pl.BlockSpec((1, N_COND, D_MODEL), lambda l, j: (l, 0, j)),
        out_shape=jax.ShapeDtypeStruct((DEPTH, N_COND, 6 * D_MODEL), F32),
        compiler_params=_cparams(2),
        name="modulation",
    )(cond, w_mod, b_mod.reshape(DEPTH, 1, 6 * D_MODEL))
    return out.reshape(DEPTH, N_COND, 6, D_MODEL)


def _in_kernel(x_ref, mod_ref, g_ref, w_ref, ab_ref, qkv_ref, kc_ref, vc_ref):
    i = pl.program_id(0)
    m = mod_ref[0, 0]
    h = _rms(x_ref[...]) * g_ref[0] * (1.0 + m[1:2]) + m[0:1]
    y = jnp.dot(h.astype(BF16), w_ref[0], preferred_element_type=F32)
    ab_ref[...] = y[:, :D_AB]
    qkv_ref[...] = y[:, D_AB:].astype(BF16)

    @pl.when(i < N_PROMPT_TM)
    def _():
        kc_ref[...] = y[:, D_AB + D_C:D_AB + 2 * D_C]
        vc_ref[...] = y[:, D_AB + 2 * D_C:]


def _in_proj(x, mod, norm1, w_in, l):
    clamp = lambda i: jnp.minimum(i, N_PROMPT_TM - 1)
    return pl.pallas_call(
        _in_kernel,
        grid=(N_TOK // TM,),
        in_specs=[pl.BlockSpec((TM, D_MODEL), lambda i: (i, 0)),
                  pl.BlockSpec((1, 1, 6, D_MODEL), lambda i: (l, _cond_of_tm_block(i), 0, 0)),
                  pl.BlockSpec((1, 1, D_MODEL), lambda i: (l, 0, 0)),
                  pl.BlockSpec((1, D_MODEL, D_IN), lambda i: (l, 0, 0), pipeline_mode=pl.Buffered(1))],
        out_specs=[pl.BlockSpec((TM, D_AB), lambda i: (i, 0)),
                   pl.BlockSpec((TM, 3 * D_C), lambda i: (i, 0)),
                   pl.BlockSpec((TM, D_C), lambda i: (clamp(i), 0)),
                   pl.BlockSpec((TM, D_C), lambda i: (clamp(i), 0))],
        out_shape=[jax.ShapeDtypeStruct((N_TOK, D_AB), F32),
                   jax.ShapeDtypeStruct((N_TOK, 3 * D_C), BF16),
                   jax.ShapeDtypeStruct((N_PROMPT, D_C), F32),
                   jax.ShapeDtypeStruct((N_PROMPT, D_C), F32)],
        compiler_params=_cparams(1),
        name="in_proj",
    )(x, mod, norm1, w_in)


def _scan_segment(a_ref, b_ref, slab, rb, carry, h_ref, hb, tile_ref, reverse):
    order = range(7, -1, -1) if reverse else range(8)
    ps, qs = {}, {}
    p = q = None
    for j in order:
        aj = a_ref[slab, pl.ds(rb + j, 32, stride=8), :]
        bj = b_ref[slab, pl.ds(rb + j, 32, stride=8), :]
        if p is None:
            p, q = aj, bj
        else:
            q = aj * q + bj
            p = aj * p
        ps[j], qs[j] = p, q
    tile_ref[0] = p
    tile_ref[1] = q
    for k in (range(31, -1, -1) if reverse else range(32)):
        tile_ref[2, k:k + 1, :] = carry
        carry = tile_ref[0, k:k + 1, :] * carry + tile_ref[1, k:k + 1, :]
    cin = tile_ref[2]
    for j in range(8):
        h_ref[slab, pl.ds(hb + j, 32, stride=8), :] = ps[j] * cin + qs[j]
    return carry


def _rglru_kernel(xa_ref, ga_ref, h0_ref, cw_ref, cb_ref, wg_ref, bg_ref, lam_ref, out_ref, fin_ref,
                  xseg, af, bf, ab, bb, hf, hseg, tiles):
    i = pl.program_id(0)
    is_prompt = i < N_PROMPT_MIX
    keep = jnp.where(is_prompt, 0.0, 1.0)
    lam = lam_ref[0]
    sp = jnp.maximum(-lam, 0.0) + jnp.log1p(jnp.exp(-jnp.abs(lam)))
    cw = cw_ref[0]
    cb = cb_ref[0]
    bg = bg_ref[0]
    h0 = h0_ref[0]

    def gates(xc, z, d):
        r = jax.nn.sigmoid(z[:, 2 * d * D_A:(2 * d + 1) * D_A])
        g = jax.nn.sigmoid(z[:, (2 * d + 1) * D_A:(2 * d + 2) * D_A])
        log_a = (-RG_C * r) * sp[d:d + 1]
        a = jnp.exp(log_a)
        return a, jnp.sqrt(-jnp.tanh(log_a) * (a * a + 1.0)) * (g * xc)

    def fwd_body(s, carry):
        c0, c1 = carry
        r0 = pl.multiple_of(s * SEG, SEG)
        prev_ok = jnp.where(s > 0, keep, 0.0)
        next_ok = jnp.where(s < N_SEG - 1, keep, 0.0)
        xseg[0:8, :] = xa_ref[pl.ds(pl.multiple_of(jnp.maximum(r0 - 8, 0), 8), 8), :] * prev_ok
        xseg[8:8 + SEG, :] = xa_ref[pl.ds(r0, SEG), :]
        xseg[8 + SEG:16 + SEG, :] = xa_ref[pl.ds(pl.multiple_of(jnp.minimum(r0 + SEG, MIX_ROWS - 8), 8), 8), :] * next_ok
        xc = cb
        for j in range(4):
            xc = xc + cw[j:j + 1] * xseg[6 + j:6 + j + SEG, :]
        z = jnp.dot(xc.astype(BF16), wg_ref[0], preferred_element_type=F32) + bg
        a_f, b_f = gates(xc, z, 0)
        a_b, b_b = gates(xc, z, 1)
        for c in range(2):
            sl = slice(c * 128, (c + 1) * 128)
            af[c] = a_f[:, sl]
            bf[c] = b_f[:, sl]
            ab[c, pl.ds(r0, SEG), :] = a_b[:, sl]
            bb[c, pl.ds(r0, SEG), :] = b_b[:, sl]
        first = s == 0
        c0 = jnp.where(first, h0[0:1, 0:128], c0) * keep
        c1 = jnp.where(first, h0[0:1, 128:256], c1) * keep
        c0 = _scan_segment(af, bf, 0, 0, c0, hf, r0, tiles.at[0], False)
        c1 = _scan_segment(af, bf, 1, 0, c1, hf, r0, tiles.at[1], False)
        fin_ref[s, 0:1, 0:128] = c0
        fin_ref[s, 0:1, 128:256] = c1
        return c0, c1

    zero = jnp.zeros((1, 128), F32)
    lax.fori_loop(0, N_SEG, fwd_body, (zero, zero))

    def bwd_body(t, carry):
        c0, c1 = carry
        s = N_SEG - 1 - t
        r0 = pl.multiple_of(s * SEG, SEG)
        first = t == 0
        c0 = jnp.where(first, h0[1:2, 0:128], c0) * keep
        c1 = jnp.where(first, h0[1:2, 128:256], c1) * keep
        c0 = _scan_segment(ab, bb, 0, r0, c0, hseg, 0, tiles.at[0], True)
        c1 = _scan_segment(ab, bb, 1, r0, c1, hseg, 0, tiles.at[1], True)
        fin_ref[s, 1:2, 0:128] = c0
        fin_ref[s, 1:2, 128:256] = c1
        for c in range(2):
            sl = slice(c * 128, (c + 1) * 128)
            ya = hf[c, pl.ds(r0, SEG), :] + hseg[c]
            out_ref[pl.ds(r0, SEG), sl] = (jax.nn.gelu(ga_ref[pl.ds(r0, SEG), sl]) * ya).astype(BF16)
        return c0, c1

    lax.fori_loop(0, N_SEG, bwd_body, (zero, zero))


def _rglru_mixer(ab_proj, h0, conv_w, conv_b, wg, bg, lam, l):
    latent = lambda i: jnp.maximum(i - N_PROMPT_MIX, 0)
    return pl.pallas_call(
        _rglru_kernel,
        grid=(N_MIX_STEPS,),
        in_specs=[pl.BlockSpec((MIX_ROWS, D_A), lambda i: (i, 0)),
                  pl.BlockSpec((MIX_ROWS, D_A), lambda i: (i, 1)),
                  pl.BlockSpec((1, 2, D_A), lambda i: (latent(i), 0, 0)),
                  pl.BlockSpec((1, 4, D_A), lambda i: (l, 0, 0)),
                  pl.BlockSpec((1, 1, D_A), lambda i: (l, 0, 0)),
                  pl.BlockSpec((1, D_A, 4 * D_A), lambda i: (l, 0, 0)),
                  pl.BlockSpec((1, 1, 4 * D_A), lambda i: (l, 0, 0)),
                  pl.BlockSpec((1, 2, D_A), lambda i: (l, 0, 0))],
        out_specs=[pl.BlockSpec((MIX_ROWS, D_A), lambda i: (i, 0)),
                   pl.BlockSpec((N_SEG, 2, D_A), lambda i: (i, 0, 0))],
        out_shape=[jax.ShapeDtypeStruct((N_TOK, D_A), BF16),
                   jax.ShapeDtypeStruct((N_MIX_STEPS * N_SEG, 2, D_A), F32)],
        scratch_shapes=[pltpu.VMEM((SEG + 16, D_A), F32),
                        pltpu.VMEM((2, SEG, 128), F32), pltpu.VMEM((2, SEG, 128), F32),
                        pltpu.VMEM((2, MIX_ROWS, 128), F32), pltpu.VMEM((2, MIX_ROWS, 128), F32),
                        pltpu.VMEM((2, MIX_ROWS, 128), F32), pltpu.VMEM((2, SEG, 128), F32),
                        pltpu.VMEM((2, 3, 32, 128), F32)],
        compiler_params=_cparams(1),
        name="rglru_mixer",
    )(ab_proj, ab_proj, h0, conv_w, conv_b, wg, bg, lam)


PAD = 16


def _split3(x):
    hi = x.astype(BF16)
    r1 = x - hi.astype(F32)
    mid = r1.astype(BF16)
    lo = (r1 - mid.astype(F32)).astype(BF16)
    return hi, mid, lo


def _hgrn_kernel(bq_ref, bf_ref, bb_ref, bi_ref, bg_ref, lb_ref, gn_ref, s0_ref, tri_ref, e_ref,
                 out_ref, fin_ref, of_scr, fpad, kpad, vpad, cum_scr):
    i = pl.program_id(0)
    keep = jnp.where(i < N_PROMPT_MIX, 0.0, 1.0)
    lb = lb_ref[0]
    ones_e = e_ref[...]
    lane = lax.broadcasted_iota(jnp.int32, (1, D_B), 1)
    head_mask = [jnp.where(lane // HG_DIM == g, 1.0, 0.0) for g in range(HG_HEADS)]
    rr = lax.broadcasted_iota(jnp.int32, (D_B, D_B), 0) // HG_DIM
    cc = lax.broadcasted_iota(jnp.int32, (D_B, D_B), 1) // HG_DIM
    bd_mask = jnp.where(rr == cc, 1.0, 0.0)
    row_in_chunk = lax.broadcasted_iota(jnp.int32, (SEG, D_B), 0) % HG_CHUNK

    zpad = jnp.zeros((PAD, D_B), F32)
    for ref in (fpad, kpad, vpad):
        ref[0:PAD, :] = zpad
        ref[PAD + SEG:2 * PAD + SEG, :] = zpad

    def one_direction(r0, d, state):
        rev = d == 1
        q = jax.nn.silu(bq_ref[pl.ds(r0, SEG), :])
        v = bi_ref[pl.ds(r0, SEG), :]
        gate = (bb_ref if rev else bf_ref)[pl.ds(r0, SEG), :]
        lbd = lb[d:d + 1]
        f = lbd + (1.0 - lbd) * jax.nn.sigmoid(gate)
        k = 1.0 - f
        logf = jnp.log(f)

        tri = tri_ref[d]
        cum = sum(jnp.dot(tri, part, preferred_element_type=F32) for part in _split3(logf))
        cum_scr[...] = cum

        edge = (HG_CHUNK - 1) if rev else 0
        fpad[PAD:PAD + SEG, :] = jnp.where(row_in_chunk == edge, 0.0, f)
        kpad[PAD:PAD + SEG, :] = k
        vpad[PAD:PAD + SEG, :] = v

        def shifted(ref, n):
            start = PAD + n if rev else PAD - n
            return ref[start:start + SEG, :]

        qd = q
        o = jnp.zeros((SEG, D_B), F32)
        for dd in range(HG_CHUNK):
            if dd > 0:
                qd = qd * shifted(fpad, dd - 1)
            p = (qd * shifted(kpad, dd)).astype(BF16)
            o = o + jnp.dot(p, ones_e, preferred_element_type=F32) * shifted(vpad, dd)

        vb = v.astype(BF16)
        b = 2 * HG_CHUNK
        while b <= SEG:
            h = b // 2
            pieces = []
            for m0 in range(0, SEG, b):
                if rev:
                    qs, ks, ref_row = slice(m0, m0 + h), slice(m0 + h, m0 + b), m0 + h
                else:
                    qs, ks, ref_row = slice(m0 + h, m0 + b), slice(m0, m0 + h), m0 + h - 1
                ref = cum_scr[ref_row:ref_row + 1, :]
                qt = q[qs] * jnp.exp(cum[qs] - ref)
                kt = (k[ks] * jnp.exp(ref - cum[ks])).astype(BF16)
                qst = jnp.concatenate([qt * head_mask[g] for g in range(HG_HEADS)], axis=0).astype(BF16)
                sc = lax.dot_general(qst, kt, (((1,), (1,)), ((), ())), preferred_element_type=F32)
                pv = jnp.dot(sc.astype(BF16), vb[ks], preferred_element_type=F32)
                ob = pv[0:h] * head_mask[0]
                for g in range(1, HG_HEADS):
                    ob = ob + pv[g * h:(g + 1) * h] * head_mask[g]
                zeros = jnp.zeros((h, D_B), F32)
                pieces += [ob, zeros] if rev else [zeros, ob]
            o = o + jnp.concatenate(pieces, axis=0)
            b *= 2

        tot_row = 0 if rev else SEG - 1
        tot = cum_scr[tot_row:tot_row + 1, :]
        qe = (q * jnp.exp(cum)).astype(BF16)
        o = o + lax.dot_general(qe, state.astype(BF16), (((1,), (1,)), ((), ())), preferred_element_type=F32)
        ke = (k * jnp.exp(tot - cum)).astype(BF16)
        ds = lax.dot_general(vb, ke, (((0,), (0,)), ((), ())), preferred_element_type=F32)
        state = state * jnp.exp(tot) + bd_mask * ds
        return o, state

    def store_state(s, d, state):
        for g in range(HG_HEADS):
            sl = slice(g * HG_DIM, (g + 1) * HG_DIM)
            fin_ref[s, d, g] = state[sl, sl]

    def fwd_body(s, state):
        r0 = pl.multiple_of(s * SEG, SEG)
        state = jnp.where(s == 0, s0_ref[0, 0], state) * keep
        o, state = one_direction(r0, 0, state)
        of_scr[pl.ds(r0, SEG), :] = o
        store_state(s, 0, state)
        return state

    lax.fori_loop(0, N_SEG, fwd_body, jnp.zeros((D_B, D_B), F32))

    def bwd_body(t, state):
        s = N_SEG - 1 - t
        r0 = pl.multiple_of(s * SEG, SEG)
        state = jnp.where(t == 0, s0_ref[0, 1], state) * keep
        o, state = one_direction(r0, 1, state)
        store_state(s, 1, state)
        o = o + of_scr[pl.ds(r0, SEG), :]
        sq = o * o
        hi = sq.astype(BF16)
        lo = (sq - hi.astype(F32)).astype(BF16)
        ms = (jnp.dot(hi, ones_e, preferred_element_type=F32)
              + jnp.dot(lo, ones_e, preferred_element_type=F32)) * (1.0 / HG_DIM)
        y = o * lax.rsqrt(ms + EPS) * gn_ref[0]
        out_ref[pl.ds(r0, SEG), :] = (y * jax.nn.silu(bg_ref[pl.ds(r0, SEG), :])).astype(BF16)
        return state

    lax.fori_loop(0, N_SEG, bwd_body, jnp.zeros((D_B, D_B), F32))


def _hgrn_constants():
    t = np.arange(SEG)
    tril = (t[:, None] >= t[None, :]).astype(np.float32)
    tri = np.stack([tril, tril.T])
    hd = np.arange(D_B) // HG_DIM
    ones_e = (hd[:, None] == hd[None, :]).astype(np.float32)
    return jnp.asarray(tri, BF16), jnp.asarray(ones_e, BF16)


def _hgrn_mixer(ab_proj, lower, hg_norm, s0t, l):
    tri, ones_e = _hgrn_constants()
    latent = lambda i: jnp.maximum(i - N_PROMPT_MIX, 0)
    col = lambda c: pl.BlockSpec((MIX_ROWS, D_B), lambda i: (i, c))
    return pl.pallas_call(
        _hgrn_kernel,
        grid=(N_MIX_STEPS,),
        in_specs=[col(2), col(3), col(4), col(5), col(6),
                  pl.BlockSpec((1, 2, D_B), lambda i: (l, 0, 0)),
                  pl.BlockSpec((1, 1, D_B), lambda i: (l, 0, 0)),
                  pl.BlockSpec((1, 2, D_B, D_B), lambda i: (latent(i), 0, 0, 0)),
                  pl.BlockSpec((2, SEG, SEG), lambda i: (0, 0, 0)),
                  pl.BlockSpec((D_B, D_B), lambda i: (0, 0))],
        out_specs=[pl.BlockSpec((MIX_ROWS, D_B), lambda i: (i, 0)),
                   pl.BlockSpec((N_SEG, 2, HG_HEADS, HG_DIM, HG_DIM), lambda i: (i, 0, 0, 0, 0))],
        out_shape=[jax.ShapeDtypeStruct((N_TOK, D_B), BF16),
                   jax.ShapeDtypeStruct((N_MIX_STEPS * N_SEG, 2, HG_HEADS, HG_DIM, HG_DIM), F32)],
        scratch_shapes=[pltpu.VMEM((MIX_ROWS, D_B), F32),
                        pltpu.VMEM((SEG + 2 * PAD, D_B), F32), pltpu.VMEM((SEG + 2 * PAD, D_B), F32),
                        pltpu.VMEM((SEG + 2 * PAD, D_B), F32), pltpu.VMEM((SEG, D_B), F32)],
        compiler_params=_cparams(1),
        name="hgrn_mixer",
    )(ab_proj, ab_proj, ab_proj, ab_proj, ab_proj, lower, hg_norm, s0t, tri, ones_e)


def _pair_queries(q2):
    lane = lax.broadcasted_iota(jnp.int32, (1, 128), 1)
    lo = jnp.where(lane < NA_DIM, NA_SCALE, 0.0)
    hi = jnp.where(lane < NA_DIM, 0.0, NA_SCALE)
    qf = q2.astype(F32)
    return jnp.concatenate([qf * lo, qf * hi], axis=0).astype(BF16)


def _pair_merge(pv, n):
    lane = lax.broadcasted_iota(jnp.int32, (1, 128), 1)
    return jnp.where(lane < NA_DIM, pv[0:n], pv[n:2 * n])


def _dot_t(a, b):
    return lax.dot_general(a, b, (((1,), (1,)), ((), ())), preferred_element_type=F32)


def _ctx_attn_kernel(q_ref, k_ref, v_ref, o_ref):
    for j in range(NA_HEADS // 2):
        sl = slice(j * 128, (j + 1) * 128)
        qs = _pair_queries(q_ref[:, sl])
        s = _dot_t(qs, k_ref[:, sl])
        e = jnp.exp(s - jnp.max(s, axis=-1, keepdims=True))
        pv = jnp.dot(e.astype(BF16), v_ref[:, sl], preferred_element_type=F32)
        pv = pv / jnp.sum(e, axis=-1, keepdims=True)
        o_ref[:, sl] = _pair_merge(pv, SEQ).astype(BF16)


def _ctx_attention(qkv):
    return pl.pallas_call(
        _ctx_attn_kernel,
        grid=(BATCH,),
        in_specs=[pl.BlockSpec((SEQ, D_C), lambda b: (b, 0)),
                  pl.BlockSpec((SEQ, D_C), lambda b: (b, 1)),
                  pl.BlockSpec((SEQ, D_C), lambda b: (b, 2))],
        out_specs=pl.BlockSpec((SEQ, D_C), lambda b: (b, 0)),
        out_shape=jax.ShapeDtypeStruct((N_PROMPT, D_C), BF16),
        compiler_params=_cparams(1),
        name="ctx_attention",
    )(qkv, qkv, qkv)


ROWS_PER_STEP = 4
N_BIAS_CLASSES = 8


def _nattn_kernel(q_ref, k_ref, v_ref, ck_ref, cv_ref, bias_ref, o_ref):
    g = pl.program_id(1)
    n_loc = WIN_R * GRID_W
    for rr in range(ROWS_PER_STEP):
        r = g * ROWS_PER_STEP + rr
        r0 = jnp.clip(r - WIN_R // 2, 0, GRID_ROWS - WIN_R)
        cls = jnp.minimum(r, 4) + jnp.maximum(r - (GRID_ROWS - 4), 0)
        start = pl.multiple_of(r0 * GRID_W, GRID_W)
        rows = slice(rr * GRID_W, (rr + 1) * GRID_W)
        for j in range(NA_HEADS // 2):
            sl = slice(j * 128, (j + 1) * 128)
            qs = _pair_queries(q_ref[rows, sl])
            bias = jnp.concatenate([bias_ref[0, 2 * j, cls], bias_ref[0, 2 * j + 1, cls]], axis=0)
            s_loc = _dot_t(qs, k_ref[pl.ds(start, n_loc), sl]) + bias
            s_ctx = _dot_t(qs, ck_ref[0, :, sl])
            m = jnp.maximum(jnp.max(s_loc, axis=-1, keepdims=True), jnp.max(s_ctx, axis=-1, keepdims=True))
            e_loc = jnp.exp(s_loc - m)
            e_ctx = jnp.exp(s_ctx - m)
            den = jnp.sum(e_loc, axis=-1, keepdims=True) + jnp.sum(e_ctx, axis=-1, keepdims=True)
            pv = (jnp.dot(e_loc.astype(BF16), v_ref[pl.ds(start, n_loc), sl], preferred_element_type=F32)
                  + jnp.dot(e_ctx.astype(BF16), cv_ref[0, :, sl], preferred_element_type=F32))
            o_ref[rows, sl] = _pair_merge(pv / den, GRID_W).astype(BF16)


def _na_bias_table(rpb):
    r = np.array([0, 1, 2, 3, 4, GRID_ROWS - 3, GRID_ROWS - 2, GRID_ROWS - 1])
    r0 = np.clip(r - WIN_R // 2, 0, GRID_ROWS - WIN_R)
    dy = r0[:, None] + np.arange(WIN_R)[None] - r[:, None] + WIN_R - 1
    col = np.arange(GRID_W)
    c0 = np.clip(col - WIN_C // 2, 0, GRID_W - WIN_C)
    col_mask = (col[None] >= c0[:, None]) & (col[None] < c0[:, None] + WIN_C)
    dx = np.clip(col[None] - col[:, None], 1 - WIN_C, WIN_C - 1) + WIN_C - 1
    bias = rpb.astype(F32)[:, :, dy[:, None, :, None], dx[None, :, None, :]]
    bias = jnp.where(jnp.asarray(col_mask)[None, None, None, :, None, :], bias, NEG_BIG)
    return bias.reshape(DEPTH, NA_HEADS, N_BIAS_CLASSES, GRID_W, WIN_R * GRID_W)


def _latent_attention(qkv, ctx_k, ctx_v, bias, l):
    n_groups = GRID_ROWS // ROWS_PER_STEP
    rows_q = ROWS_PER_STEP * GRID_W
    q_blocks_before = N_PROMPT // rows_q
    seq_blocks_before = N_PROMPT // DEC_SEQ
    return pl.pallas_call(
        _nattn_kernel,
        grid=(DEC_BATCH, n_groups),
        in_specs=[pl.BlockSpec((rows_q, D_C), lambda b, g: (q_blocks_before + b * n_groups + g, 0)),
                  pl.BlockSpec((DEC_SEQ, D_C), lambda b, g: (seq_blocks_before + b, 1)),
                  pl.BlockSpec((DEC_SEQ, D_C), lambda b, g: (seq_blocks_before + b, 2)),
                  pl.BlockSpec((1, PAST_LEN, D_C), lambda b, g: (b * DEPTH + l, 0, 0)),
                  pl.BlockSpec((1, PAST_LEN, D_C), lambda b, g: (b * DEPTH + l, 0, 0)),
                  pl.BlockSpec((1, NA_HEADS, N_BIAS_CLASSES, GRID_W, WIN_R * GRID_W),
                               lambda b, g: (l, 0, 0, 0, 0), pipeline_mode=pl.Buffered(1))],
        out_specs=pl.BlockSpec((rows_q, D_C), lambda b, g: (b * n_groups + g, 0)),
        out_shape=jax.ShapeDtypeStruct((DEC_BATCH * DEC_SEQ, D_C), BF16),
        compiler_params=_cparams(2),
        name="latent_attention",
    )(qkv, qkv, qkv, ctx_k, ctx_v, bias)


def _post_kernel(a_ref, b_ref, cp_ref, cs_ref, x_ref, mod_ref, g2_ref, wo_ref, w1_ref, w2_ref, nf_ref,
                 o_ref, cat, *, final):
    i = pl.program_id(0)
    cat[:, 0:D_A] = a_ref[...]
    cat[:, D_A:D_A + D_B] = b_ref[...]

    @pl.when(i < N_PROMPT_TM)
    def _():
        cat[:, D_A + D_B:] = cp_ref[...]

    @pl.when(i >= N_PROMPT_TM)
    def _():
        cat[:, D_A + D_B:] = cs_ref[...]

    m = mod_ref[0, 0]
    mix = jnp.dot(cat[...], wo_ref[0], preferred_element_type=F32)
    x1 = x_ref[...] + m[2:3] * mix
    h2 = (_rms(x1) * g2_ref[0] * (1.0 + m[4:5]) + m[3:4]).astype(BF16)
    acc = jnp.zeros((TM, D_MODEL), F32)
    for c in range(D_FF // D_MODEL):
        cols = slice(c * D_MODEL, (c + 1) * D_MODEL)
        u = jnp.dot(h2, w1_ref[0, :, cols], preferred_element_type=F32)
        u = jnp.square(jnp.maximum(u, 0.0)).astype(BF16)
        acc = acc + jnp.dot(u, w2_ref[0, cols, :], preferred_element_type=F32)
    x2 = x1 + m[5:6] * acc
    if final:
        x2 = _rms(x2) * nf_ref[...]
    o_ref[...] = x2


def _post(out_a, out_b, out_cp, out_cs, x, mod, norm2, w_out, w1, w2, norm_f, l, final):
    n_s = (N_TOK - N_PROMPT) // TM
    const = lambda shape: pl.BlockSpec(shape, lambda i: (l,) + (0,) * (len(shape) - 1),
                                       pipeline_mode=pl.Buffered(1))
    return pl.pallas_call(
        functools.partial(_post_kernel, final=final),
        grid=(N_TOK // TM,),
        in_specs=[pl.BlockSpec((TM, D_A), lambda i: (i, 0)),
                  pl.BlockSpec((TM, D_B), lambda i: (i, 0)),
                  pl.BlockSpec((TM, D_C), lambda i: (jnp.minimum(i, N_PROMPT_TM - 1), 0)),
                  pl.BlockSpec((TM, D_C), lambda i: (jnp.clip(i - N_PROMPT_TM, 0, n_s - 1), 0)),
                  pl.BlockSpec((TM, D_MODEL), lambda i: (i, 0)),
                  pl.BlockSpec((1, 1, 6, D_MODEL), lambda i: (l, _cond_of_tm_block(i), 0, 0)),
                  pl.BlockSpec((1, 1, D_MODEL), lambda i: (l, 0, 0)),
                  const((1, D_MODEL, D_MODEL)),
                  const((1, D_MODEL, D_FF)),
                  const((1, D_FF, D_MODEL)),
                  pl.BlockSpec((1, D_MODEL), lambda i: (0, 0))],
        out_specs=pl.BlockSpec((TM, D_MODEL), lambda i: (i, 0)),
        out_shape=jax.ShapeDtypeStruct((N_TOK, D_MODEL), F32),
        scratch_shapes=[pltpu.VMEM((TM, D_MODEL), BF16)],
        compiler_params=_cparams(1),
        name="post_mlp",
    )(out_a, out_b, out_cp, out_cs, x, mod, norm2, w_out, w1, w2, norm_f)


def _block_diag(w):
    n, d, e = w.shape[-3:]
    eye = jnp.eye(n, dtype=w.dtype)
    full = w[..., :, :, None, :] * eye[:, None, :, None]
    return full.reshape(w.shape[:-3] + (n * d, n * e))


def kernel(x_prompt, x_sample, cache_k, cache_v, state_rglru, state_hgrn, c, c_ctx, w_mod, b_mod, norm1, norm2,
           w_in, rg_conv_w, rg_conv_b, rg_w_a, rg_b_a, rg_w_x, rg_b_x, rg_lambda, hg_lb, hg_norm, na_rpb,
           w_out, w1, w2, norm_f):
    cond = jnp.concatenate([c_ctx[None], c, jnp.zeros((N_COND - 1 - DEC_BATCH, D_MODEL), F32)], axis=0)
    w_in_b, w_out_b, w1_b, w2_b = (w.astype(BF16) for w in (w_in, w_out, w1, w2))
    wg = jnp.concatenate([_block_diag(rg_w_a[:, 0]), _block_diag(rg_w_x[:, 0]),
                          _block_diag(rg_w_a[:, 1]), _block_diag(rg_w_x[:, 1])], axis=-1).astype(BF16)
    bg = jnp.concatenate([rg_b_a[:, 0], rg_b_x[:, 0], rg_b_a[:, 1], rg_b_x[:, 1]], axis=-1)[:, None, :]
    lb_w = jax.nn.softmax(hg_lb.astype(F32), axis=0)
    hg_lower = jnp.cumsum(lb_w, axis=0) - lb_w[0]
    s0t = _block_diag(jnp.swapaxes(state_hgrn.astype(F32), -1, -2))
    ctx_k = cache_k.reshape(DEC_BATCH * DEPTH, PAST_LEN, D_C).astype(BF16)
    ctx_v = cache_v.reshape(DEC_BATCH * DEPTH, PAST_LEN, D_C).astype(BF16)
    bias = _na_bias_table(na_rpb)
    norm1_3, norm2_3 = norm1[:, None, :], norm2[:, None, :]
    conv_b3, hg_norm3 = rg_conv_b[:, None, :], hg_norm[:, None, :]
    norm_f2 = norm_f[None, :]

    mod = _modulation(cond, w_mod, b_mod)
    x = jnp.concatenate([x_prompt.reshape(N_PROMPT, D_MODEL), x_sample.reshape(N_TOK - N_PROMPT, D_MODEL)], axis=0)

    new_k, new_v, new_rg, new_hg = [], [], [], []
    for l in range(DEPTH):
        ab_proj, qkv, k_new, v_new = _in_proj(x, mod, norm1_3, w_in_b, l)
        out_a, fin_a = _rglru_mixer(ab_proj, state_rglru[:, l].astype(F32), rg_conv_w, conv_b3, wg, bg, rg_lambda, l)
        out_b, fin_b = _hgrn_mixer(ab_proj, hg_lower, hg_norm3, s0t[:, l], l)
        out_cp = _ctx_attention(qkv)
        out_cs = _latent_attention(qkv, ctx_k, ctx_v, bias, l)
        x = _post(out_a, out_b, out_cp, out_cs, x, mod, norm2_3, w_out_b, w1_b, w2_b, norm_f2, l,
                  final=(l == DEPTH - 1))
        new_k.append(k_new.reshape(BATCH, SEQ, NA_HEADS, NA_DIM))
        new_v.append(v_new.reshape(BATCH, SEQ, NA_HEADS, NA_DIM))
        new_rg.append(fin_a[:BATCH])
        new_hg.append(jnp.swapaxes(fin_b[:BATCH], -1, -2))

    y_prompt = x[:N_PROMPT].reshape(BATCH, SEQ, D_MODEL)
    y_sample = x[N_PROMPT:].reshape(DEC_BATCH, DEC_SEQ, D_MODEL)
    return (y_prompt, y_sample, jnp.stack(new_k, axis=1), jnp.stack(new_v, axis=1),
            jnp.stack(new_rg, axis=1), jnp.stack(new_hg, axis=1))
```

```python
import functools

import numpy as np
import jax
import jax.numpy as jnp
from jax import lax
from jax.experimental import pallas as pl
from jax.experimental.pallas import tpu as pltpu

F32 = jnp.float32
BF16 = jnp.bfloat16

D_MODEL = 1024
BATCH = 32
SEQ = 256
DEPTH = 4
DEC_BATCH = 4
DEC_SEQ = 2048
PAST_LEN = 512
GRID_W = 64
GRID_ROWS = DEC_SEQ // GRID_W
D_A = 256
RG_BLOCKS = 4
RG_BLOCK_DIM = 64
RG_C = 8.0
D_B = 256
HG_HEADS = 4
HG_DIM = 64
HG_CHUNK = 16
D_C = 512
NA_DIM = 64
NA_HEADS = 8
NA_SCALE = NA_DIM ** -0.5
WIN_R = 8
WIN_C = 16
D_FF = 4 * D_MODEL
D_AB = 2 * D_A + 5 * D_B
D_IN = D_AB + 3 * D_C
EPS = 1e-6

N_PROMPT = BATCH * SEQ
N_TOK = N_PROMPT + DEC_BATCH * DEC_SEQ
TM = 512
N_PROMPT_TM = N_PROMPT // TM
MIX_ROWS = 2048
N_MIX_STEPS = N_TOK // MIX_ROWS
N_PROMPT_MIX = N_PROMPT // MIX_ROWS
SEG = 256
N_SEG = MIX_ROWS // SEG
N_COND = 8
NEG_BIG = -1e30

VMEM_LIMIT = 56 * 1024 * 1024


def _cparams(n_axes):
    return pltpu.CompilerParams(dimension_semantics=("arbitrary",) * n_axes,
                                vmem_limit_bytes=VMEM_LIMIT)


def _cond_of_tm_block(i):
    return jnp.where(i < N_PROMPT_TM, 0, 1 + (i - N_PROMPT_TM) // (DEC_SEQ // TM))


def _rms(x):
    return x * lax.rsqrt(jnp.mean(x * x, axis=-1, keepdims=True) + EPS)


def _mod_kernel(cond_ref, w_ref, b_ref, o_ref):
    s = jax.nn.silu(cond_ref[...])
    o_ref[0] = jnp.dot(s.astype(BF16), w_ref[0].astype(BF16), preferred_element_type=F32) + b_ref[0]


def _modulation(cond, w_mod, b_mod):
    nb = 6
    out = pl.pallas_call(
        _mod_kernel,
        grid=(DEPTH, nb),
        in_specs=[pl.BlockSpec((N_COND, D_MODEL), lambda l, j: (0, 0)),
                  pl.BlockSpec((1, D_MODEL, D_MODEL), lambda l, j: (l, 0, j)),
                  pl.BlockSpec((1, 1, D_MODEL), lambda l, j: (l, 0, j))],
        out_specs=pl.BlockSpec((1, N_COND, D_MODEL), lambda l, j: (l, 0, j)),
        out_shape=jax.ShapeDtypeStruct((DEPTH, N_COND, 6 * D_MODEL), F32),
        compiler_params=_cparams(2),
        name="modulation",
    )(cond, w_mod, b_mod.reshape(DEPTH, 1, 6 * D_MODEL))
    return out.reshape(DEPTH, N_COND, 6, D_MODEL)


def _prompt_tm_block(i):
    return jnp.minimum(i, N_PROMPT_TM - 1)


def _latent_tm_block(i):
    return jnp.clip(i - N_PROMPT_TM, 0, (N_TOK - N_PROMPT) // TM - 1)


def _in_kernel(xp_ref, xs_ref, mod_ref, g_ref, w_ref, kin_ref, vin_ref, ab_ref, qkv_ref, kc_ref, vc_ref):
    del kin_ref, vin_ref
    i = pl.program_id(0)
    m = mod_ref[0, 0]
    x = jnp.where(i < N_PROMPT_TM, xp_ref[...], xs_ref[...])
    h = _rms(x) * g_ref[0] * (1.0 + m[1:2]) + m[0:1]
    y = jnp.dot(h.astype(BF16), w_ref[0], preferred_element_type=F32)
    ab_ref[...] = y[:, :D_AB]
    qkv_ref[...] = y[:, D_AB:].astype(BF16)

    @pl.when(i < N_PROMPT_TM)
    def _():
        for j in range(TM // SEQ):
            rows = slice(j * SEQ, (j + 1) * SEQ)
            kc_ref[j, 0] = y[rows, D_AB + D_C:D_AB + 2 * D_C]
            vc_ref[j, 0] = y[rows, D_AB + 2 * D_C:]


def _in_proj(xp, xs, mod, norm1, w_in, k_buf, v_buf, l):
    cache_spec = pl.BlockSpec((TM // SEQ, 1, SEQ, D_C), lambda i: (_prompt_tm_block(i), l, 0, 0))
    cache_shape = jax.ShapeDtypeStruct((BATCH, DEPTH, SEQ, D_C), F32)
    return pl.pallas_call(
        _in_kernel,
        grid=(N_TOK // TM,),
        in_specs=[pl.BlockSpec((TM, D_MODEL), lambda i: (_prompt_tm_block(i), 0)),
                  pl.BlockSpec((TM, D_MODEL), lambda i: (_latent_tm_block(i), 0)),
                  pl.BlockSpec((1, 1, 6, D_MODEL), lambda i: (l, _cond_of_tm_block(i), 0, 0)),
                  pl.BlockSpec((1, 1, D_MODEL), lambda i: (l, 0, 0)),
                  pl.BlockSpec((1, D_MODEL, D_IN), lambda i: (l, 0, 0), pipeline_mode=pl.Buffered(1)),
                  pl.BlockSpec(memory_space=pl.ANY),
                  pl.BlockSpec(memory_space=pl.ANY)],
        out_specs=[pl.BlockSpec((TM, D_AB), lambda i: (i, 0)),
                   pl.BlockSpec((TM, 3 * D_C), lambda i: (i, 0)),
                   cache_spec, cache_spec],
        out_shape=[jax.ShapeDtypeStruct((N_TOK, D_AB), F32),
                   jax.ShapeDtypeStruct((N_TOK, 3 * D_C), BF16),
                   cache_shape, cache_shape],
        input_output_aliases={5: 2, 6: 3},
        compiler_params=_cparams(1),
        name="in_proj",
    )(xp, xs, mod, norm1, w_in, k_buf, v_buf)


def _scan_segment(a_ref, b_ref, slab, rb, carry, h_ref, hb, tile_ref, reverse):
    order = range(7, -1, -1) if reverse else range(8)
    ps, qs = {}, {}
    p = q = None
    for j in order:
        aj = a_ref[slab, pl.ds(rb + j, 32, stride=8), :]
        bj = b_ref[slab, pl.ds(rb + j, 32, stride=8), :]
        if p is None:
            p, q = aj, bj
        else:
            q = aj * q + bj
            p = aj * p
        ps[j], qs[j] = p, q
    tile_ref[0] = p
    tile_ref[1] = q
    for k in (range(31, -1, -1) if reverse else range(32)):
        tile_ref[2, k:k + 1, :] = carry
        carry = tile_ref[0, k:k + 1, :] * carry + tile_ref[1, k:k + 1, :]
    cin = tile_ref[2]
    for j in range(8):
        h_ref[slab, pl.ds(hb + j, 32, stride=8), :] = ps[j] * cin + qs[j]
    return carry


def _rglru_kernel(xa_ref, ga_ref, h0_ref, cw_ref, cb_ref, wg_ref, bg_ref, lam_ref, out_ref, fin_ref,
                  xseg, af, bf, ab, bb, hf, hseg, tiles):
    i = pl.program_id(0)
    is_prompt = i < N_PROMPT_MIX
    keep = jnp.where(is_prompt, 0.0, 1.0)
    lam = lam_ref[0]
    sp = jnp.maximum(-lam, 0.0) + jnp.log1p(jnp.exp(-jnp.abs(lam)))
    cw = cw_ref[0]
    cb = cb_ref[0]
    bg = bg_ref[0]
    h0 = h0_ref[0]

    def gates(xc, z, d):
        r = jax.nn.sigmoid(z[:, 2 * d * D_A:(2 * d + 1) * D_A])
        g = jax.nn.sigmoid(z[:, (2 * d + 1) * D_A:(2 * d + 2) * D_A])
        log_a = (-RG_C * r) * sp[d:d + 1]
        a = jnp.exp(log_a)
        return a, jnp.sqrt(-jnp.tanh(log_a) * (a * a + 1.0)) * (g * xc)

    def fwd_body(s, carry):
        c0, c1 = carry
        r0 = pl.multiple_of(s * SEG, SEG)
        prev_ok = jnp.where(s > 0, keep, 0.0)
        next_ok = jnp.where(s < N_SEG - 1, keep, 0.0)
        xseg[0:8, :] = xa_ref[pl.ds(pl.multiple_of(jnp.maximum(r0 - 8, 0), 8), 8), :] * prev_ok
        xseg[8:8 + SEG, :] = xa_ref[pl.ds(r0, SEG), :]
        xseg[8 + SEG:16 + SEG, :] = xa_ref[pl.ds(pl.multiple_of(jnp.minimum(r0 + SEG, MIX_ROWS - 8), 8), 8), :] * next_ok
        xc = cb
        for j in range(4):
            xc = xc + cw[j:j + 1] * xseg[6 + j:6 + j + SEG, :]
        z = jnp.dot(xc.astype(BF16), wg_ref[0], preferred_element_type=F32) + bg
        a_f, b_f = gates(xc, z, 0)
        a_b, b_b = gates(xc, z, 1)
        for c in range(2):
            sl = slice(c * 128, (c + 1) * 128)
            af[c] = a_f[:, sl]
            bf[c] = b_f[:, sl]
            ab[c, pl.ds(r0, SEG), :] = a_b[:, sl]
            bb[c, pl.ds(r0, SEG), :] = b_b[:, sl]
        first = s == 0
        c0 = jnp.where(first, h0[0:1, 0:128], c0) * keep
        c1 = jnp.where(first, h0[0:1, 128:256], c1) * keep
        c0 = _scan_segment(af, bf, 0, 0, c0, hf, r0, tiles.at[0], False)
        c1 = _scan_segment(af, bf, 1, 0, c1, hf, r0, tiles.at[1], False)
        fin_ref[s, 0:1, 0:128] = c0
        fin_ref[s, 0:1, 128:256] = c1
        return c0, c1

    zero = jnp.zeros((1, 128), F32)
    lax.fori_loop(0, N_SEG, fwd_body, (zero, zero))

    def bwd_body(t, carry):
        c0, c1 = carry
        s = N_SEG - 1 - t
        r0 = pl.multiple_of(s * SEG, SEG)
        first = t == 0
        c0 = jnp.where(first, h0[1:2, 0:128], c0) * keep
        c1 = jnp.where(first, h0[1:2, 128:256], c1) * keep
        c0 = _scan_segment(ab, bb, 0, r0, c0, hseg, 0, tiles.at[0], True)
        c1 = _scan_segment(ab, bb, 1, r0, c1, hseg, 0, tiles.at[1], True)
        fin_ref[s, 1:2, 0:128] = c0
        fin_ref[s, 1:2, 128:256] = c1
        for c in range(2):
            sl = slice(c * 128, (c + 1) * 128)
            ya = hf[c, pl.ds(r0, SEG), :] + hseg[c]
            out_ref[pl.ds(r0, SEG), sl] = (jax.nn.gelu(ga_ref[pl.ds(r0, SEG), sl]) * ya).astype(BF16)
        return c0, c1

    lax.fori_loop(0, N_SEG, bwd_body, (zero, zero))


def _rglru_mixer(ab_proj, h0, conv_w, conv_b, wg, bg, lam, l):
    latent = lambda i: jnp.maximum(i - N_PROMPT_MIX, 0)
    return pl.pallas_call(
        _rglru_kernel,
        grid=(N_MIX_STEPS,),
        in_specs=[pl.BlockSpec((MIX_ROWS, D_A), lambda i: (i, 0)),
                  pl.BlockSpec((MIX_ROWS, D_A), lambda i: (i, 1)),
                  pl.BlockSpec((1, 2, D_A), lambda i: (latent(i), 0, 0)),
                  pl.BlockSpec((1, 4, D_A), lambda i: (l, 0, 0)),
                  pl.BlockSpec((1, 1, D_A), lambda i: (l, 0, 0)),
                  pl.BlockSpec((1, D_A, 4 * D_A), lambda i: (l, 0, 0)),
                  pl.BlockSpec((1, 1, 4 * D_A), lambda i: (l, 0, 0)),
                  pl.BlockSpec((1, 2, D_A), lambda i: (l, 0, 0))],
        out_specs=[pl.BlockSpec((MIX_ROWS, D_A), lambda i: (i, 0)),
                   pl.BlockSpec((N_SEG, 2, D_A), lambda i: (i, 0, 0))],
        out_shape=[jax.ShapeDtypeStruct((N_TOK, D_A), BF16),
                   jax.ShapeDtypeStruct((N_MIX_STEPS * N_SEG, 2, D_A), F32)],
        scratch_shapes=[pltpu.VMEM((SEG + 16, D_A), F32),
                        pltpu.VMEM((2, SEG, 128), F32), pltpu.VMEM((2, SEG, 128), F32),
                        pltpu.VMEM((2, MIX_ROWS, 128), F32), pltpu.VMEM((2, MIX_ROWS, 128), F32),
                        pltpu.VMEM((2, MIX_ROWS, 128), F32), pltpu.VMEM((2, SEG, 128), F32),
                        pltpu.VMEM((2, 3, 32, 128), F32)],
        compiler_params=_cparams(1),
        name="rglru_mixer",
    )(ab_proj, ab_proj, h0, conv_w, conv_b, wg, bg, lam)


def _split3(x):
    hi = x.astype(BF16)
    r1 = x - hi.astype(F32)
    mid = r1.astype(BF16)
    lo = (r1 - mid.astype(F32)).astype(BF16)
    return hi, mid, lo


def _hgrn_kernel(bq0, bq1, bf0, bf1, bb0, bb1, bi0, bi1, bg0, bg1, lb_ref, gn_ref, s0_ref, tri_ref, e_ref,
                 fin_in_ref, out_ref, fin_ref, of_scr, o_nat, cum_scr):
    del fin_in_ref
    i = pl.program_id(0)
    keep = jnp.where(i < N_PROMPT_MIX, 0.0, 1.0)
    lb = lb_ref[0]
    ones_e = e_ref[...]
    lane = lax.broadcasted_iota(jnp.int32, (1, D_B), 1)
    head_mask = [jnp.where(lane // HG_DIM == g, 1.0, 0.0) for g in range(HG_HEADS)]
    rr = lax.broadcasted_iota(jnp.int32, (D_B, D_B), 0) // HG_DIM
    cc = lax.broadcasted_iota(jnp.int32, (D_B, D_B), 1) // HG_DIM
    bd_mask = jnp.where(rr == cc, 1.0, 0.0)
    n_chunks = SEG // HG_CHUNK

    def natural(refs, r0):
        return jnp.concatenate([ref[pl.ds(r0, SEG), :] for ref in refs], axis=1)

    def by_phase(refs, r0):
        return jnp.concatenate(
            [jnp.concatenate([ref[pl.ds(r0 + tau, n_chunks, stride=HG_CHUNK), :] for ref in refs], axis=1)
             for tau in range(HG_CHUNK)], axis=0)

    def one_direction(r0, d, state):
        rev = d == 1
        gate_refs = (bb0, bb1) if rev else (bf0, bf1)
        lbd = lb[d:d + 1]

        qd = jax.nn.silu(by_phase((bq0, bq1), r0))
        fp = lbd + (1.0 - lbd) * jax.nn.sigmoid(by_phase(gate_refs, r0))
        kp = 1.0 - fp
        vp = by_phase((bi0, bi1), r0)
        o_p = jnp.zeros((SEG, D_B), F32)
        for dd in range(HG_CHUNK):
            m = (HG_CHUNK - dd) * n_chunks
            lo = dd * n_chunks
            if rev:
                if dd > 0:
                    qd = qd[:m] * fp[lo - n_chunks:lo - n_chunks + m]
                p = (qd * kp[lo:]).astype(BF16)
                add = jnp.dot(p, ones_e, preferred_element_type=F32) * vp[lo:]
                o_p = jnp.concatenate([o_p[:m] + add, o_p[m:]], axis=0) if dd > 0 else o_p + add
            else:
                if dd > 0:
                    qd = qd[n_chunks:] * fp[n_chunks:n_chunks + m]
                p = (qd * kp[:m]).astype(BF16)
                add = jnp.dot(p, ones_e, preferred_element_type=F32) * vp[:m]
                o_p = jnp.concatenate([o_p[:lo], o_p[lo:] + add], axis=0) if dd > 0 else o_p + add
        for tau in range(HG_CHUNK):
            for c in range(2):
                o_nat[c, pl.ds(tau, n_chunks, stride=HG_CHUNK), :] = (
                    o_p[tau * n_chunks:(tau + 1) * n_chunks, c * 128:(c + 1) * 128])
        o = jnp.concatenate([o_nat[0], o_nat[1]], axis=1)

        q = jax.nn.silu(natural((bq0, bq1), r0))
        v = natural((bi0, bi1), r0)
        f = lbd + (1.0 - lbd) * jax.nn.sigmoid(natural(gate_refs, r0))
        k = 1.0 - f
        logf = jnp.log(f)

        tri = tri_ref[d]
        cum = sum(jnp.dot(tri, part, preferred_element_type=F32) for part in _split3(logf))
        cum_scr[...] = cum

        vb = v.astype(BF16)
        b = 2 * HG_CHUNK
        while b <= SEG:
            h = b // 2
            pieces = []
            for m0 in range(0, SEG, b):
                if rev:
                    qs, ks, ref_row = slice(m0, m0 + h), slice(m0 + h, m0 + b), m0 + h
                else:
                    qs, ks, ref_row = slice(m0 + h, m0 + b), slice(m0, m0 + h), m0 + h - 1
                ref = cum_scr[ref_row:ref_row + 1, :]
                qt = q[qs] * jnp.exp(cum[qs] - ref)
                kt = (k[ks] * jnp.exp(ref - cum[ks])).astype(BF16)
                qst = jnp.concatenate([qt * head_mask[g] for g in range(HG_HEADS)], axis=0).astype(BF16)
                sc = lax.dot_general(qst, kt, (((1,), (1,)), ((), ())), preferred_element_type=F32)
                pv = jnp.dot(sc.astype(BF16), vb[ks], preferred_element_type=F32)
                ob = pv[0:h] * head_mask[0]
                for g in range(1, HG_HEADS):
                    ob = ob + pv[g * h:(g + 1) * h] * head_mask[g]
                zeros = jnp.zeros((h, D_B), F32)
                pieces += [ob, zeros] if rev else [zeros, ob]
            o = o + jnp.concatenate(pieces, axis=0)
            b *= 2

        tot_row = 0 if rev else SEG - 1
        tot = cum_scr[tot_row:tot_row + 1, :]
        qe = (q * jnp.exp(cum)).astype(BF16)
        o = o + lax.dot_general(qe, state.astype(BF16), (((1,), (1,)), ((), ())), preferred_element_type=F32)
        ke = (k * jnp.exp(tot - cum)).astype(BF16)
        ds = lax.dot_general(vb, ke, (((0,), (0,)), ((), ())), preferred_element_type=F32)
        state = state * jnp.exp(tot) + bd_mask * ds
        return o, state

    def store_state(s, d, state):
        @pl.when(i < N_PROMPT_MIX)
        def _():
            st = state.T
            for g in range(HG_HEADS):
                sl = slice(g * HG_DIM, (g + 1) * HG_DIM)
                fin_ref[s, 0, d, g] = st[sl, sl]

    def fwd_body(s, state):
        r0 = pl.multiple_of(s * SEG, SEG)
        state = jnp.where(s == 0, s0_ref[0, 0], state) * keep
        o, state = one_direction(r0, 0, state)
        of_scr[pl.ds(r0, SEG), :] = o
        store_state(s, 0, state)
        return state

    lax.fori_loop(0, N_SEG, fwd_body, jnp.zeros((D_B, D_B), F32))

    def bwd_body(t, state):
        s = N_SEG - 1 - t
        r0 = pl.multiple_of(s * SEG, SEG)
        state = jnp.where(t == 0, s0_ref[0, 1], state) * keep
        o, state = one_direction(r0, 1, state)
        store_state(s, 1, state)
        o = o + of_scr[pl.ds(r0, SEG), :]
        sq = o * o
        hi = sq.astype(BF16)
        lo = (sq - hi.astype(F32)).astype(BF16)
        ms = (jnp.dot(hi, ones_e, preferred_element_type=F32)
              + jnp.dot(lo, ones_e, preferred_element_type=F32)) * (1.0 / HG_DIM)
        y = o * lax.rsqrt(ms + EPS) * gn_ref[0]
        out_ref[pl.ds(r0, SEG), :] = (y * jax.nn.silu(natural((bg0, bg1), r0))).astype(BF16)
        return state

    lax.fori_loop(0, N_SEG, bwd_body, jnp.zeros((D_B, D_B), F32))


def _hgrn_constants():
    t = np.arange(SEG)
    tril = (t[:, None] >= t[None, :]).astype(np.float32)
    tri = np.stack([tril, tril.T])
    hd = np.arange(D_B) // HG_DIM
    ones_e = (hd[:, None] == hd[None, :]).astype(np.float32)
    return jnp.asarray(tri, BF16), jnp.asarray(ones_e, BF16)


def _hgrn_mixer(ab_proj, lower, hg_norm, s0t, fin_buf, l):
    tri, ones_e = _hgrn_constants()
    latent = lambda i: jnp.maximum(i - N_PROMPT_MIX, 0)
    first_slab = 2 * D_A // 128
    slabs = [pl.BlockSpec((MIX_ROWS, 128), lambda i, c=c: (i, c)) for c in range(first_slab, first_slab + 10)]
    return pl.pallas_call(
        _hgrn_kernel,
        grid=(N_MIX_STEPS,),
        in_specs=slabs + [
                  pl.BlockSpec((1, 2, D_B), lambda i: (l, 0, 0)),
                  pl.BlockSpec((1, 1, D_B), lambda i: (l, 0, 0)),
                  pl.BlockSpec((1, 2, D_B, D_B), lambda i: (latent(i), 0, 0, 0)),
                  pl.BlockSpec((2, SEG, SEG), lambda i: (0, 0, 0)),
                  pl.BlockSpec((D_B, D_B), lambda i: (0, 0)),
                  pl.BlockSpec(memory_space=pl.ANY)],
        out_specs=[pl.BlockSpec((MIX_ROWS, D_B), lambda i: (i, 0)),
                   pl.BlockSpec((N_SEG, 1, 2, HG_HEADS, HG_DIM, HG_DIM),
                                lambda i: (jnp.minimum(i, N_PROMPT_MIX - 1), l, 0, 0, 0, 0))],
        out_shape=[jax.ShapeDtypeStruct((N_TOK, D_B), BF16),
                   jax.ShapeDtypeStruct((BATCH, DEPTH, 2, HG_HEADS, HG_DIM, HG_DIM), F32)],
        input_output_aliases={15: 1},
        scratch_shapes=[pltpu.VMEM((MIX_ROWS, D_B), F32),
                        pltpu.VMEM((2, SEG, 128), F32), pltpu.VMEM((SEG, D_B), F32)],
        compiler_params=_cparams(1),
        name="hgrn_mixer",
    )(*([ab_proj] * 10), lower, hg_norm, s0t, tri, ones_e, fin_buf)


def _pair_queries(q2):
    lane = lax.broadcasted_iota(jnp.int32, (1, 128), 1)
    lo = jnp.where(lane < NA_DIM, NA_SCALE, 0.0)
    hi = jnp.where(lane < NA_DIM, 0.0, NA_SCALE)
    qf = q2.astype(F32)
    return jnp.concatenate([qf * lo, qf * hi], axis=0).astype(BF16)


def _pair_merge(pv, n):
    lane = lax.broadcasted_iota(jnp.int32, (1, 128), 1)
    return jnp.where(lane < NA_DIM, pv[0:n], pv[n:2 * n])


def _dot_t(a, b):
    return lax.dot_general(a, b, (((1,), (1,)), ((), ())), preferred_element_type=F32)


def _ctx_attn_kernel(q_ref, k_ref, v_ref, o_ref):
    for j in range(NA_HEADS // 2):
        sl = slice(j * 128, (j + 1) * 128)
        qs = _pair_queries(q_ref[:, sl])
        s = _dot_t(qs, k_ref[:, sl])
        e = jnp.exp(s - jnp.max(s, axis=-1, keepdims=True))
        pv = jnp.dot(e.astype(BF16), v_ref[:, sl], preferred_element_type=F32)
        pv = pv / jnp.sum(e, axis=-1, keepdims=True)
        o_ref[:, sl] = _pair_merge(pv, SEQ).astype(BF16)


def _ctx_attention(qkv):
    return pl.pallas_call(
        _ctx_attn_kernel,
        grid=(BATCH,),
        in_specs=[pl.BlockSpec((SEQ, D_C), lambda b: (b, 0)),
                  pl.BlockSpec((SEQ, D_C), lambda b: (b, 1)),
                  pl.BlockSpec((SEQ, D_C), lambda b: (b, 2))],
        out_specs=pl.BlockSpec((SEQ, D_C), lambda b: (b, 0)),
        out_shape=jax.ShapeDtypeStruct((N_PROMPT, D_C), BF16),
        compiler_params=_cparams(1),
        name="ctx_attention",
    )(qkv, qkv, qkv)


ROWS_PER_STEP = 4
N_BIAS_CLASSES = 8


def _nattn_kernel(q_ref, k_ref, v_ref, ck_ref, cv_ref, bias_ref, o_ref):
    g = pl.program_id(1)
    n_loc = WIN_R * GRID_W
    for rr in range(ROWS_PER_STEP):
        r = g * ROWS_PER_STEP + rr
        r0 = jnp.clip(r - WIN_R // 2, 0, GRID_ROWS - WIN_R)
        cls = jnp.minimum(r, 4) + jnp.maximum(r - (GRID_ROWS - 4), 0)
        start = pl.multiple_of(r0 * GRID_W, GRID_W)
        rows = slice(rr * GRID_W, (rr + 1) * GRID_W)
        for j in range(NA_HEADS // 2):
            sl = slice(j * 128, (j + 1) * 128)
            qs = _pair_queries(q_ref[rows, sl])
            bias = jnp.concatenate([bias_ref[0, 2 * j, cls], bias_ref[0, 2 * j + 1, cls]], axis=0)
            s_loc = _dot_t(qs, k_ref[pl.ds(start, n_loc), sl]) + bias
            s_ctx = _dot_t(qs, ck_ref[0, :, sl])
            m = jnp.maximum(jnp.max(s_loc, axis=-1, keepdims=True), jnp.max(s_ctx, axis=-1, keepdims=True))
            e_loc = jnp.exp(s_loc - m)
            e_ctx = jnp.exp(s_ctx - m)
            den = jnp.sum(e_loc, axis=-1, keepdims=True) + jnp.sum(e_ctx, axis=-1, keepdims=True)
            pv = (jnp.dot(e_loc.astype(BF16), v_ref[pl.ds(start, n_loc), sl], preferred_element_type=F32)
                  + jnp.dot(e_ctx.astype(BF16), cv_ref[0, :, sl], preferred_element_type=F32))
            o_ref[rows, sl] = _pair_merge(pv / den, GRID_W).astype(BF16)


def _na_bias_table(rpb):
    r = np.array([0, 1, 2, 3, 4, GRID_ROWS - 3, GRID_ROWS - 2, GRID_ROWS - 1])
    r0 = np.clip(r - WIN_R // 2, 0, GRID_ROWS - WIN_R)
    dy = r0[:, None] + np.arange(WIN_R)[None] - r[:, None] + WIN_R - 1
    col = np.arange(GRID_W)
    c0 = np.clip(col - WIN_C // 2, 0, GRID_W - WIN_C)
    col_mask = (col[None] >= c0[:, None]) & (col[None] < c0[:, None] + WIN_C)
    dx = np.clip(col[None] - col[:, None], 1 - WIN_C, WIN_C - 1) + WIN_C - 1
    onehot = jnp.asarray(dx[None] == np.arange(2 * WIN_C - 1)[:, None, None], F32)
    by_dy = jnp.einsum('lhyx,xqk->lhyqk', rpb.astype(F32), onehot, precision=lax.Precision.HIGHEST)
    by_dy = jnp.where(jnp.asarray(col_mask), by_dy, NEG_BIG)
    per_class = [jnp.concatenate([by_dy[:, :, int(y)] for y in dy_c], axis=-1) for dy_c in dy]
    return jnp.stack(per_class, axis=2)


def _latent_attention(qkv, ctx_k, ctx_v, bias, l):
    n_groups = GRID_ROWS // ROWS_PER_STEP
    rows_q = ROWS_PER_STEP * GRID_W
    q_blocks_before = N_PROMPT // rows_q
    seq_blocks_before = N_PROMPT // DEC_SEQ
    return pl.pallas_call(
        _nattn_kernel,
        grid=(DEC_BATCH, n_groups),
        in_specs=[pl.BlockSpec((rows_q, D_C), lambda b, g: (q_blocks_before + b * n_groups + g, 0)),
                  pl.BlockSpec((DEC_SEQ, D_C), lambda b, g: (seq_blocks_before + b, 1)),
                  pl.BlockSpec((DEC_SEQ, D_C), lambda b, g: (seq_blocks_before + b, 2)),
                  pl.BlockSpec((1, PAST_LEN, D_C), lambda b, g: (b * DEPTH + l, 0, 0)),
                  pl.BlockSpec((1, PAST_LEN, D_C), lambda b, g: (b * DEPTH + l, 0, 0)),
                  pl.BlockSpec((1, NA_HEADS, N_BIAS_CLASSES, GRID_W, WIN_R * GRID_W),
                               lambda b, g: (l, 0, 0, 0, 0), pipeline_mode=pl.Buffered(1))],
        out_specs=pl.BlockSpec((rows_q, D_C), lambda b, g: (b * n_groups + g, 0)),
        out_shape=jax.ShapeDtypeStruct((DEC_BATCH * DEC_SEQ, D_C), BF16),
        compiler_params=_cparams(2),
        name="latent_attention",
    )(qkv, qkv, qkv, ctx_k, ctx_v, bias)


def _post_kernel(a_ref, b_ref, cp_ref, cs_ref, xp_ref, xs_ref, mod_ref, g2_ref, wo_ref, w1_ref, w2_ref, nf_ref,
                 op_ref, os_ref, cat, *, final):
    i = pl.program_id(0)
    is_prompt = i < N_PROMPT_TM
    cat[:, 0:D_A] = a_ref[...]
    cat[:, D_A:D_A + D_B] = b_ref[...]

    @pl.when(is_prompt)
    def _():
        cat[:, D_A + D_B:] = cp_ref[...]

    @pl.when(jnp.logical_not(is_prompt))
    def _():
        cat[:, D_A + D_B:] = cs_ref[...]

    m = mod_ref[0, 0]
    mix = jnp.dot(cat[...], wo_ref[0], preferred_element_type=F32)
    x1 = jnp.where(is_prompt, xp_ref[...], xs_ref[...]) + m[2:3] * mix
    h2 = (_rms(x1) * g2_ref[0] * (1.0 + m[4:5]) + m[3:4]).astype(BF16)
    acc = jnp.zeros((TM, D_MODEL), F32)
    for c in range(D_FF // D_MODEL):
        cols = slice(c * D_MODEL, (c + 1) * D_MODEL)
        u = jnp.dot(h2, w1_ref[0, :, cols], preferred_element_type=F32)
        u = jnp.square(jnp.maximum(u, 0.0)).astype(BF16)
        acc = acc + jnp.dot(u, w2_ref[0, cols, :], preferred_element_type=F32)
    x2 = x1 + m[5:6] * acc
    if final:
        x2 = _rms(x2) * nf_ref[...]

    @pl.when(is_prompt)
    def _():
        op_ref[...] = x2

    @pl.when(jnp.logical_not(is_prompt))
    def _():
        os_ref[...] = x2


def _post(out_a, out_b, out_cp, out_cs, xp, xs, mod, norm2, w_out, w1, w2, norm_f, l, final):
    const = lambda shape: pl.BlockSpec(shape, lambda i: (l,) + (0,) * (len(shape) - 1),
                                       pipeline_mode=pl.Buffered(1))
    prompt_rows = lambda width: pl.BlockSpec((TM, width), lambda i: (_prompt_tm_block(i), 0))
    latent_rows = lambda width: pl.BlockSpec((TM, width), lambda i: (_latent_tm_block(i), 0))
    return pl.pallas_call(
        functools.partial(_post_kernel, final=final),
        grid=(N_TOK // TM,),
        in_specs=[pl.BlockSpec((TM, D_A), lambda i: (i, 0)),
                  pl.BlockSpec((TM, D_B), lambda i: (i, 0)),
                  prompt_rows(D_C), latent_rows(D_C),
                  prompt_rows(D_MODEL), latent_rows(D_MODEL),
                  pl.BlockSpec((1, 1, 6, D_MODEL), lambda i: (l, _cond_of_tm_block(i), 0, 0)),
                  pl.BlockSpec((1, 1, D_MODEL), lambda i: (l, 0, 0)),
                  const((1, D_MODEL, D_MODEL)),
                  const((1, D_MODEL, D_FF)),
                  const((1, D_FF, D_MODEL)),
                  pl.BlockSpec((1, D_MODEL), lambda i: (0, 0))],
        out_specs=[prompt_rows(D_MODEL), latent_rows(D_MODEL)],
        out_shape=[jax.ShapeDtypeStruct((N_PROMPT, D_MODEL), F32),
                   jax.ShapeDtypeStruct((N_TOK - N_PROMPT, D_MODEL), F32)],
        scratch_shapes=[pltpu.VMEM((TM, D_MODEL), BF16)],
        compiler_params=_cparams(1),
        name="post_mlp",
    )(out_a, out_b, out_cp, out_cs, xp, xs, mod, norm2, w_out, w1, w2, norm_f)


def _block_diag(w):
    n, d, e = w.shape[-3:]
    eye = jnp.eye(n, dtype=w.dtype)
    full = w[..., :, :, None, :] * eye[:, None, :, None]
    return full.reshape(w.shape[:-3] + (n * d, n * e))


def kernel(x_prompt, x_sample, cache_k, cache_v, state_rglru, state_hgrn, c, c_ctx, w_mod, b_mod, norm1, norm2,
           w_in, rg_conv_w, rg_conv_b, rg_w_a, rg_b_a, rg_w_x, rg_b_x, rg_lambda, hg_lb, hg_norm, na_rpb,
           w_out, w1, w2, norm_f):
    cond = jnp.concatenate([c_ctx[None], c, jnp.zeros((N_COND - 1 - DEC_BATCH, D_MODEL), F32)], axis=0)
    w_in_b, w_out_b, w1_b, w2_b = (w.astype(BF16) for w in (w_in, w_out, w1, w2))
    wg = jnp.concatenate([_block_diag(rg_w_a[:, 0]), _block_diag(rg_w_x[:, 0]),
                          _block_diag(rg_w_a[:, 1]), _block_diag(rg_w_x[:, 1])], axis=-1).astype(BF16)
    bg = jnp.concatenate([rg_b_a[:, 0], rg_b_x[:, 0], rg_b_a[:, 1], rg_b_x[:, 1]], axis=-1)[:, None, :]
    lb_w = jax.nn.softmax(hg_lb.astype(F32), axis=0)
    hg_lower = jnp.cumsum(lb_w, axis=0) - lb_w[0]
    s0t = _block_diag(jnp.swapaxes(state_hgrn.astype(F32), -1, -2))
    ctx_k = cache_k.reshape(DEC_BATCH * DEPTH, PAST_LEN, D_C).astype(BF16)
    ctx_v = cache_v.reshape(DEC_BATCH * DEPTH, PAST_LEN, D_C).astype(BF16)
    bias = _na_bias_table(na_rpb)
    norm1_3, norm2_3 = norm1[:, None, :], norm2[:, None, :]
    conv_b3, hg_norm3 = rg_conv_b[:, None, :], hg_norm[:, None, :]
    norm_f2 = norm_f[None, :]

    mod = _modulation(cond, w_mod, b_mod)
    xp = x_prompt.reshape(N_PROMPT, D_MODEL)
    xs = x_sample.reshape(N_TOK - N_PROMPT, D_MODEL)

    new_k = jnp.zeros((BATCH, DEPTH, SEQ, D_C), F32)
    new_v = jnp.zeros((BATCH, DEPTH, SEQ, D_C), F32)
    new_hg = jnp.zeros((BATCH, DEPTH, 2, HG_HEADS, HG_DIM, HG_DIM), F32)
    new_rg = []
    for l in range(DEPTH):
        ab_proj, qkv, new_k, new_v = _in_proj(xp, xs, mod, norm1_3, w_in_b, new_k, new_v, l)
        out_a, fin_a = _rglru_mixer(ab_proj, state_rglru[:, l].astype(F32), rg_conv_w, conv_b3, wg, bg, rg_lambda, l)
        out_b, new_hg = _hgrn_mixer(ab_proj, hg_lower, hg_norm3, s0t[:, l], new_hg, l)
        out_cp = _ctx_attention(qkv)
        out_cs = _latent_attention(qkv, ctx_k, ctx_v, bias, l)
        xp, xs = _post(out_a, out_b, out_cp, out_cs, xp, xs, mod, norm2_3, w_out_b, w1_b, w2_b, norm_f2, l,
                       final=(l == DEPTH - 1))
        new_rg.append(fin_a[:BATCH])

    return (xp.reshape(BATCH, SEQ, D_MODEL), xs.reshape(DEC_BATCH, DEC_SEQ, D_MODEL),
            new_k.reshape(BATCH, DEPTH, SEQ, NA_HEADS, NA_DIM), new_v.reshape(BATCH, DEPTH, SEQ, NA_HEADS, NA_DIM),
            jnp.stack(new_rg, axis=1), new_hg)
```

```python
import functools

import numpy as np
import jax
import jax.numpy as jnp
from jax import lax
from jax.experimental import pallas as pl
from jax.experimental.pallas import tpu as pltpu

F32 = jnp.float32
BF16 = jnp.bfloat16

D_MODEL = 1024
BATCH = 32
SEQ = 256
DEPTH = 4
DEC_BATCH = 4
DEC_SEQ = 2048
PAST_LEN = 512
GRID_W = 64
GRID_ROWS = DEC_SEQ // GRID_W
D_A = 256
RG_BLOCKS = 4
RG_BLOCK_DIM = 64
RG_C = 8.0
D_B = 256
HG_HEADS = 4
HG_DIM = 64
HG_CHUNK = 16
D_C = 512
NA_DIM = 64
NA_HEADS = 8
NA_SCALE = NA_DIM ** -0.5
WIN_R = 8
WIN_C = 16
D_FF = 4 * D_MODEL
D_AB = 2 * D_A + 5 * D_B
D_IN = D_AB + 3 * D_C
EPS = 1e-6

N_PROMPT = BATCH * SEQ
N_TOK = N_PROMPT + DEC_BATCH * DEC_SEQ
TM = 512
N_PROMPT_TM = N_PROMPT // TM
MIX_ROWS = 2048
N_MIX_STEPS = N_TOK // MIX_ROWS
N_PROMPT_MIX = N_PROMPT // MIX_ROWS
SEG = 256
N_SEG = MIX_ROWS // SEG
N_COND = 8
NEG_BIG = -1e30

VMEM_LIMIT = 56 * 1024 * 1024


def _cparams(n_axes):
    return pltpu.CompilerParams(dimension_semantics=("arbitrary",) * n_axes,
                                vmem_limit_bytes=VMEM_LIMIT)


def _cond_of_tm_block(i):
    return jnp.where(i < N_PROMPT_TM, 0, 1 + (i - N_PROMPT_TM) // (DEC_SEQ // TM))


def _rms(x):
    return x * lax.rsqrt(jnp.mean(x * x, axis=-1, keepdims=True) + EPS)


def _mod_kernel(cond_ref, w_ref, b_ref, o_ref):
    s = jax.nn.silu(cond_ref[...])
    o_ref[0] = jnp.dot(s.astype(BF16), w_ref[0].astype(BF16), preferred_element_type=F32) + b_ref[0]


def _modulation(cond, w_mod, b_mod):
    nb = 6
    out = pl.pallas_call(
        _mod_kernel,
        grid=(DEPTH, nb),
        in_specs=[pl.BlockSpec((N_COND, D_MODEL), lambda l, j: (0, 0)),
                  pl.BlockSpec((1, D_MODEL, D_MODEL), lambda l, j: (l, 0, j)),
                  pl.BlockSpec((1, 1, D_MODEL), lambda l, j: (l, 0, j))],
        out_specs=pl.BlockSpec((1, N_COND, D_MODEL), lambda l, j: (l, 0, j)),
        out_shape=jax.ShapeDtypeStruct((DEPTH, N_COND, 6 * D_MODEL), F32),
        compiler_params=_cparams(2),
        name="modulation",
    )(cond, w_mod, b_mod.reshape(DEPTH, 1, 6 * D_MODEL))
    return out.reshape(DEPTH, N_COND, 6, D_MODEL)


def _prompt_tm_block(i):
    return jnp.minimum(i, N_PROMPT_TM - 1)


def _latent_tm_block(i):
    return jnp.clip(i - N_PROMPT_TM, 0, (N_TOK - N_PROMPT) // TM - 1)


def _in_kernel(xp_ref, xs_ref, mod_ref, g_ref, w_ref, kin_ref, vin_ref, ab_ref, qkv_ref, kc_ref, vc_ref):
    del kin_ref, vin_ref
    i = pl.program_id(0)
    m = mod_ref[0, 0]
    x = jnp.where(i < N_PROMPT_TM, xp_ref[...], xs_ref[...])
    h = _rms(x) * g_ref[0] * (1.0 + m[1:2]) + m[0:1]
    y = jnp.dot(h.astype(BF16), w_ref[0], preferred_element_type=F32)
    ab_ref[...] = y[:, :D_AB]
    qkv_ref[...] = y[:, D_AB:].astype(BF16)

    @pl.when(i < N_PROMPT_TM)
    def _():
        for j in range(TM // SEQ):
            rows = slice(j * SEQ, (j + 1) * SEQ)
            kc_ref[j, 0] = y[rows, D_AB + D_C:D_AB + 2 * D_C]
            vc_ref[j, 0] = y[rows, D_AB + 2 * D_C:]


def _in_proj(xp, xs, mod, norm1, w_in, k_buf, v_buf, l):
    cache_spec = pl.BlockSpec((TM // SEQ, 1, SEQ, D_C), lambda i: (_prompt_tm_block(i), l, 0, 0))
    cache_shape = jax.ShapeDtypeStruct((BATCH, DEPTH, SEQ, D_C), F32)
    return pl.pallas_call(
        _in_kernel,
        grid=(N_TOK // TM,),
        in_specs=[pl.BlockSpec((TM, D_MODEL), lambda i: (_prompt_tm_block(i), 0)),
                  pl.BlockSpec((TM, D_MODEL), lambda i: (_latent_tm_block(i), 0)),
                  pl.BlockSpec((1, 1, 6, D_MODEL), lambda i: (l, _cond_of_tm_block(i), 0, 0)),
                  pl.BlockSpec((1, 1, D_MODEL), lambda i: (l, 0, 0)),
                  pl.BlockSpec((1, D_MODEL, D_IN), lambda i: (l, 0, 0), pipeline_mode=pl.Buffered(1)),
                  pl.BlockSpec(memory_space=pl.ANY),
                  pl.BlockSpec(memory_space=pl.ANY)],
        out_specs=[pl.BlockSpec((TM, D_AB), lambda i: (i, 0)),
                   pl.BlockSpec((TM, 3 * D_C), lambda i: (i, 0)),
                   cache_spec, cache_spec],
        out_shape=[jax.ShapeDtypeStruct((N_TOK, D_AB), F32),
                   jax.ShapeDtypeStruct((N_TOK, 3 * D_C), BF16),
                   cache_shape, cache_shape],
        input_output_aliases={5: 2, 6: 3},
        compiler_params=_cparams(1),
        name="in_proj",
    )(xp, xs, mod, norm1, w_in, k_buf, v_buf)


def _scan_segment(a_ref, b_ref, slab, rb, carry, h_ref, hb, tile_ref, reverse):
    order = range(7, -1, -1) if reverse else range(8)
    ps, qs = {}, {}
    p = q = None
    for j in order:
        aj = a_ref[slab, pl.ds(rb + j, 32, stride=8), :]
        bj = b_ref[slab, pl.ds(rb + j, 32, stride=8), :]
        if p is None:
            p, q = aj, bj
        else:
            q = aj * q + bj
            p = aj * p
        ps[j], qs[j] = p, q
    tile_ref[0] = p
    tile_ref[1] = q
    for k in (range(31, -1, -1) if reverse else range(32)):
        tile_ref[2, k:k + 1, :] = carry
        carry = tile_ref[0, k:k + 1, :] * carry + tile_ref[1, k:k + 1, :]
    cin = tile_ref[2]
    for j in range(8):
        h_ref[slab, pl.ds(hb + j, 32, stride=8), :] = ps[j] * cin + qs[j]
    return carry


def _rglru_kernel(xa_ref, ga_ref, h0_ref, cw_ref, cb_ref, wg_ref, bg_ref, lam_ref, out_ref, fin_ref,
                  xseg, af, bf, ab, bb, hf, hseg, tiles):
    i = pl.program_id(0)
    is_prompt = i < N_PROMPT_MIX
    keep = jnp.where(is_prompt, 0.0, 1.0)
    lam = lam_ref[0]
    sp = jnp.maximum(-lam, 0.0) + jnp.log1p(jnp.exp(-jnp.abs(lam)))
    cw = cw_ref[0]
    cb = cb_ref[0]
    bg = bg_ref[0]
    h0 = h0_ref[0]

    def gates(xc, z, d):
        r = jax.nn.sigmoid(z[:, 2 * d * D_A:(2 * d + 1) * D_A])
        g = jax.nn.sigmoid(z[:, (2 * d + 1) * D_A:(2 * d + 2) * D_A])
        log_a = (-RG_C * r) * sp[d:d + 1]
        a = jnp.exp(log_a)
        return a, jnp.sqrt(-jnp.tanh(log_a) * (a * a + 1.0)) * (g * xc)

    def fwd_body(s, carry):
        c0, c1 = carry
        r0 = pl.multiple_of(s * SEG, SEG)
        prev_ok = jnp.where(s > 0, keep, 0.0)
        next_ok = jnp.where(s < N_SEG - 1, keep, 0.0)
        xseg[0:8, :] = xa_ref[pl.ds(pl.multiple_of(jnp.maximum(r0 - 8, 0), 8), 8), :] * prev_ok
        xseg[8:8 + SEG, :] = xa_ref[pl.ds(r0, SEG), :]
        xseg[8 + SEG:16 + SEG, :] = xa_ref[pl.ds(pl.multiple_of(jnp.minimum(r0 + SEG, MIX_ROWS - 8), 8), 8), :] * next_ok
        xc = cb
        for j in range(4):
            xc = xc + cw[j:j + 1] * xseg[6 + j:6 + j + SEG, :]
        z = jnp.dot(xc.astype(BF16), wg_ref[0], preferred_element_type=F32) + bg
        a_f, b_f = gates(xc, z, 0)
        a_b, b_b = gates(xc, z, 1)
        for c in range(2):
            sl = slice(c * 128, (c + 1) * 128)
            af[c] = a_f[:, sl]
            bf[c] = b_f[:, sl]
            ab[c, pl.ds(r0, SEG), :] = a_b[:, sl]
            bb[c, pl.ds(r0, SEG), :] = b_b[:, sl]
        first = s == 0
        c0 = jnp.where(first, h0[0:1, 0:128], c0) * keep
        c1 = jnp.where(first, h0[0:1, 128:256], c1) * keep
        c0 = _scan_segment(af, bf, 0, 0, c0, hf, r0, tiles.at[0], False)
        c1 = _scan_segment(af, bf, 1, 0, c1, hf, r0, tiles.at[1], False)
        fin_ref[s, 0:1, 0:128] = c0
        fin_ref[s, 0:1, 128:256] = c1
        return c0, c1

    zero = jnp.zeros((1, 128), F32)
    lax.fori_loop(0, N_SEG, fwd_body, (zero, zero))

    def bwd_body(t, carry):
        c0, c1 = carry
        s = N_SEG - 1 - t
        r0 = pl.multiple_of(s * SEG, SEG)
        first = t == 0
        c0 = jnp.where(first, h0[1:2, 0:128], c0) * keep
        c1 = jnp.where(first, h0[1:2, 128:256], c1) * keep
        c0 = _scan_segment(ab, bb, 0, r0, c0, hseg, 0, tiles.at[0], True)
        c1 = _scan_segment(ab, bb, 1, r0, c1, hseg, 0, tiles.at[1], True)
        fin_ref[s, 1:2, 0:128] = c0
        fin_ref[s, 1:2, 128:256] = c1
        for c in range(2):
            sl = slice(c * 128, (c + 1) * 128)
            ya = hf[c, pl.ds(r0, SEG), :] + hseg[c]
            out_ref[pl.ds(r0, SEG), sl] = (jax.nn.gelu(ga_ref[pl.ds(r0, SEG), sl]) * ya).astype(BF16)
        return c0, c1

    lax.fori_loop(0, N_SEG, bwd_body, (zero, zero))


def _rglru_mixer(ab_proj, h0, conv_w, conv_b, wg, bg, lam, l):
    latent = lambda i: jnp.maximum(i - N_PROMPT_MIX, 0)
    return pl.pallas_call(
        _rglru_kernel,
        grid=(N_MIX_STEPS,),
        in_specs=[pl.BlockSpec((MIX_ROWS, D_A), lambda i: (i, 0)),
                  pl.BlockSpec((MIX_ROWS, D_A), lambda i: (i, 1)),
                  pl.BlockSpec((1, 2, D_A), lambda i: (latent(i), 0, 0)),
                  pl.BlockSpec((1, 4, D_A), lambda i: (l, 0, 0)),
                  pl.BlockSpec((1, 1, D_A), lambda i: (l, 0, 0)),
                  pl.BlockSpec((1, D_A, 4 * D_A), lambda i: (l, 0, 0)),
                  pl.BlockSpec((1, 1, 4 * D_A), lambda i: (l, 0, 0)),
                  pl.BlockSpec((1, 2, D_A), lambda i: (l, 0, 0))],
        out_specs=[pl.BlockSpec((MIX_ROWS, D_A), lambda i: (i, 0)),
                   pl.BlockSpec((N_SEG, 2, D_A), lambda i: (i, 0, 0))],
        out_shape=[jax.ShapeDtypeStruct((N_TOK, D_A), BF16),
                   jax.ShapeDtypeStruct((N_MIX_STEPS * N_SEG, 2, D_A), F32)],
        scratch_shapes=[pltpu.VMEM((SEG + 16, D_A), F32),
                        pltpu.VMEM((2, SEG, 128), F32), pltpu.VMEM((2, SEG, 128), F32),
                        pltpu.VMEM((2, MIX_ROWS, 128), F32), pltpu.VMEM((2, MIX_ROWS, 128), F32),
                        pltpu.VMEM((2, MIX_ROWS, 128), F32), pltpu.VMEM((2, SEG, 128), F32),
                        pltpu.VMEM((2, 3, 32, 128), F32)],
        compiler_params=_cparams(1),
        name="rglru_mixer",
    )(ab_proj, ab_proj, h0, conv_w, conv_b, wg, bg, lam)


def _split3(x):
    hi = x.astype(BF16)
    r1 = x - hi.astype(F32)
    mid = r1.astype(BF16)
    lo = (r1 - mid.astype(F32)).astype(BF16)
    return hi, mid, lo


def _hgrn_kernel(bq0, bq1, bf0, bf1, bb0, bb1, bi0, bi1, bg0, bg1, lb_ref, gn_ref, s0_ref, tri_ref, e_ref,
                 fin_in_ref, out_ref, fin_ref, of_scr, o_nat, cum_scr):
    del fin_in_ref
    i = pl.program_id(0)
    keep = jnp.where(i < N_PROMPT_MIX, 0.0, 1.0)
    lb = lb_ref[0]
    ones_e = e_ref[...]
    lane = lax.broadcasted_iota(jnp.int32, (1, D_B), 1)
    head_mask = [jnp.where(lane // HG_DIM == g, 1.0, 0.0) for g in range(HG_HEADS)]
    rr = lax.broadcasted_iota(jnp.int32, (D_B, D_B), 0) // HG_DIM
    cc = lax.broadcasted_iota(jnp.int32, (D_B, D_B), 1) // HG_DIM
    bd_mask = jnp.where(rr == cc, 1.0, 0.0)
    n_chunks = SEG // HG_CHUNK

    def natural(refs, r0):
        return jnp.concatenate([ref[pl.ds(r0, SEG), :] for ref in refs], axis=1)

    def by_phase(refs, r0):
        return jnp.concatenate(
            [jnp.concatenate([ref[pl.ds(r0 + tau, n_chunks, stride=HG_CHUNK), :] for ref in refs], axis=1)
             for tau in range(HG_CHUNK)], axis=0)

    def one_direction(r0, d, state):
        rev = d == 1
        gate_refs = (bb0, bb1) if rev else (bf0, bf1)
        lbd = lb[d:d + 1]

        qd = jax.nn.silu(by_phase((bq0, bq1), r0))
        fp = lbd + (1.0 - lbd) * jax.nn.sigmoid(by_phase(gate_refs, r0))
        kp = 1.0 - fp
        vp = by_phase((bi0, bi1), r0)
        p_blocks, offsets = [], []
        for dd in range(HG_CHUNK):
            m = (HG_CHUNK - dd) * n_chunks
            lo = dd * n_chunks
            if rev:
                if dd > 0:
                    qd = qd[:m] * fp[lo - n_chunks:lo - n_chunks + m]
                p_blocks.append((qd * kp[lo:]).astype(BF16))
            else:
                if dd > 0:
                    qd = qd[n_chunks:] * fp[n_chunks:n_chunks + m]
                p_blocks.append((qd * kp[:m]).astype(BF16))
            offsets.append(sum(blk.shape[0] for blk in p_blocks[:-1]))
        sums = jnp.dot(jnp.concatenate(p_blocks, axis=0), ones_e, preferred_element_type=F32)

        def group(x, g):
            return x[g * n_chunks:(g + 1) * n_chunks]

        for tau in range(HG_CHUNK):
            acc = None
            for dd in range(HG_CHUNK - tau if rev else tau + 1):
                src = tau + dd if rev else tau - dd
                row = offsets[dd] + (tau if rev else src) * n_chunks
                term = sums[row:row + n_chunks] * group(vp, src)
                acc = term if acc is None else acc + term
            for c in range(2):
                o_nat[c, pl.ds(tau, n_chunks, stride=HG_CHUNK), :] = acc[:, c * 128:(c + 1) * 128]
        o = jnp.concatenate([o_nat[0], o_nat[1]], axis=1)

        q = jax.nn.silu(natural((bq0, bq1), r0))
        v = natural((bi0, bi1), r0)
        f = lbd + (1.0 - lbd) * jax.nn.sigmoid(natural(gate_refs, r0))
        k = 1.0 - f
        logf = jnp.log(f)

        tri = tri_ref[d]
        parts = jnp.dot(tri, jnp.concatenate(_split3(logf), axis=1), preferred_element_type=F32)
        cum = parts[:, :D_B] + parts[:, D_B:2 * D_B] + parts[:, 2 * D_B:]
        cum_scr[...] = cum

        vb = v.astype(BF16)
        half = SEG // 2
        levels = []
        b = 2 * HG_CHUNK
        while b <= SEG:
            h = b // 2
            qs_l, ks_l, dq, dk = [], [], [], []
            for m0 in range(0, SEG, b):
                if rev:
                    qs, ks, ref_row = slice(m0, m0 + h), slice(m0 + h, m0 + b), m0 + h
                else:
                    qs, ks, ref_row = slice(m0 + h, m0 + b), slice(m0, m0 + h), m0 + h - 1
                ref = cum_scr[ref_row:ref_row + 1, :]
                qs_l.append(qs)
                ks_l.append(ks)
                dq.append(cum[qs] - ref)
                dk.append(ref - cum[ks])
            qt = jnp.concatenate([q[s] for s in qs_l], axis=0) * jnp.exp(jnp.concatenate(dq, axis=0))
            kt = (jnp.concatenate([k[s] for s in ks_l], axis=0) * jnp.exp(jnp.concatenate(dk, axis=0))).astype(BF16)
            qst = jnp.concatenate([qt * head_mask[g] for g in range(HG_HEADS)], axis=0).astype(BF16)
            levels.append((h, qst, kt, jnp.concatenate([vb[s] for s in ks_l], axis=0)))
            b *= 2
        scores = [lax.dot_general(qst, kt, (((1,), (1,)), ((), ())), preferred_element_type=F32)
                  for _, qst, kt, _ in levels]
        qi = lax.broadcasted_iota(jnp.int32, (HG_HEADS * half, half), 0) % half
        kj = lax.broadcasted_iota(jnp.int32, (HG_HEADS * half, half), 1)
        pvs = []
        for (h, _, _, vk), sc in zip(levels, scores):
            if h < half:
                sc = jnp.where(qi // h == kj // h, sc, 0.0)
            pvs.append(jnp.dot(sc.astype(BF16), vk, preferred_element_type=F32))
        for (h, _, _, _), pv in zip(levels, pvs):
            ob = pv[0:half] * head_mask[0]
            for g in range(1, HG_HEADS):
                ob = ob + pv[g * half:(g + 1) * half] * head_mask[g]
            zeros = jnp.zeros((h, D_B), F32)
            pieces = []
            for j in range(half // h):
                blk = ob[j * h:(j + 1) * h]
                pieces += [blk, zeros] if rev else [zeros, blk]
            o = o + jnp.concatenate(pieces, axis=0)

        tot_row = 0 if rev else SEG - 1
        tot = cum_scr[tot_row:tot_row + 1, :]
        qe = (q * jnp.exp(cum)).astype(BF16)
        o = o + lax.dot_general(qe, state.astype(BF16), (((1,), (1,)), ((), ())), preferred_element_type=F32)
        ke = (k * jnp.exp(tot - cum)).astype(BF16)
        ds = lax.dot_general(vb, ke, (((0,), (0,)), ((), ())), preferred_element_type=F32)
        state = state * jnp.exp(tot) + bd_mask * ds
        return o, state

    def store_state(s, d, state):
        @pl.when(i < N_PROMPT_MIX)
        def _():
            st = state.T
            for g in range(HG_HEADS):
                sl = slice(g * HG_DIM, (g + 1) * HG_DIM)
                fin_ref[s, 0, d, g] = st[sl, sl]

    def fwd_body(s, state):
        r0 = pl.multiple_of(s * SEG, SEG)
        state = jnp.where(s == 0, s0_ref[0, 0], state) * keep
        o, state = one_direction(r0, 0, state)
        of_scr[pl.ds(r0, SEG), :] = o
        store_state(s, 0, state)
        return state

    lax.fori_loop(0, N_SEG, fwd_body, jnp.zeros((D_B, D_B), F32))

    def bwd_body(t, state):
        s = N_SEG - 1 - t
        r0 = pl.multiple_of(s * SEG, SEG)
        state = jnp.where(t == 0, s0_ref[0, 1], state) * keep
        o, state = one_direction(r0, 1, state)
        store_state(s, 1, state)
        o = o + of_scr[pl.ds(r0, SEG), :]
        sq = o * o
        hi = sq.astype(BF16)
        lo = (sq - hi.astype(F32)).astype(BF16)
        both = jnp.dot(jnp.concatenate([hi, lo], axis=0), ones_e, preferred_element_type=F32)
        ms = (both[:SEG] + both[SEG:]) * (1.0 / HG_DIM)
        y = o * lax.rsqrt(ms + EPS) * gn_ref[0]
        out_ref[pl.ds(r0, SEG), :] = (y * jax.nn.silu(natural((bg0, bg1), r0))).astype(BF16)
        return state

    lax.fori_loop(0, N_SEG, bwd_body, jnp.zeros((D_B, D_B), F32))


def _hgrn_constants():
    t = np.arange(SEG)
    tril = (t[:, None] >= t[None, :]).astype(np.float32)
    tri = np.stack([tril, tril.T])
    hd = np.arange(D_B) // HG_DIM
    ones_e = (hd[:, None] == hd[None, :]).astype(np.float32)
    return jnp.asarray(tri, BF16), jnp.asarray(ones_e, BF16)


def _hgrn_mixer(ab_proj, lower, hg_norm, s0t, fin_buf, l):
    tri, ones_e = _hgrn_constants()
    latent = lambda i: jnp.maximum(i - N_PROMPT_MIX, 0)
    first_slab = 2 * D_A // 128
    slabs = [pl.BlockSpec((MIX_ROWS, 128), lambda i, c=c: (i, c)) for c in range(first_slab, first_slab + 10)]
    return pl.pallas_call(
        _hgrn_kernel,
        grid=(N_MIX_STEPS,),
        in_specs=slabs + [
                  pl.BlockSpec((1, 2, D_B), lambda i: (l, 0, 0)),
                  pl.BlockSpec((1, 1, D_B), lambda i: (l, 0, 0)),
                  pl.BlockSpec((1, 2, D_B, D_B), lambda i: (latent(i), 0, 0, 0)),
                  pl.BlockSpec((2, SEG, SEG), lambda i: (0, 0, 0)),
                  pl.BlockSpec((D_B, D_B), lambda i: (0, 0)),
                  pl.BlockSpec(memory_space=pl.ANY)],
        out_specs=[pl.BlockSpec((MIX_ROWS, D_B), lambda i: (i, 0)),
                   pl.BlockSpec((N_SEG, 1, 2, HG_HEADS, HG_DIM, HG_DIM),
                                lambda i: (jnp.minimum(i, N_PROMPT_MIX - 1), l, 0, 0, 0, 0))],
        out_shape=[jax.ShapeDtypeStruct((N_TOK, D_B), BF16),
                   jax.ShapeDtypeStruct((BATCH, DEPTH, 2, HG_HEADS, HG_DIM, HG_DIM), F32)],
        input_output_aliases={15: 1},
        scratch_shapes=[pltpu.VMEM((MIX_ROWS, D_B), F32),
                        pltpu.VMEM((2, SEG, 128), F32), pltpu.VMEM((SEG, D_B), F32)],
        compiler_params=_cparams(1),
        name="hgrn_mixer",
    )(*([ab_proj] * 10), lower, hg_norm, s0t, tri, ones_e, fin_buf)


def _pair_queries(q2):
    lane = lax.broadcasted_iota(jnp.int32, (1, 128), 1)
    lo = jnp.where(lane < NA_DIM, NA_SCALE, 0.0)
    hi = jnp.where(lane < NA_DIM, 0.0, NA_SCALE)
    qf = q2.astype(F32)
    return jnp.concatenate([qf * lo, qf * hi], axis=0).astype(BF16)


def _pair_merge(pv, n):
    lane = lax.broadcasted_iota(jnp.int32, (1, 128), 1)
    return jnp.where(lane < NA_DIM, pv[0:n], pv[n:2 * n])


def _dot_t(a, b):
    return lax.dot_general(a, b, (((1,), (1,)), ((), ())), preferred_element_type=F32)


def _ctx_attn_kernel(q_ref, k_ref, v_ref, o_ref):
    for j in range(NA_HEADS // 2):
        sl = slice(j * 128, (j + 1) * 128)
        qs = _pair_queries(q_ref[:, sl])
        s = _dot_t(qs, k_ref[:, sl])
        e = jnp.exp(s - jnp.max(s, axis=-1, keepdims=True))
        pv = jnp.dot(e.astype(BF16), v_ref[:, sl], preferred_element_type=F32)
        pv = pv / jnp.sum(e, axis=-1, keepdims=True)
        o_ref[:, sl] = _pair_merge(pv, SEQ).astype(BF16)


def _ctx_attention(qkv):
    return pl.pallas_call(
        _ctx_attn_kernel,
        grid=(BATCH,),
        in_specs=[pl.BlockSpec((SEQ, D_C), lambda b: (b, 0)),
                  pl.BlockSpec((SEQ, D_C), lambda b: (b, 1)),
                  pl.BlockSpec((SEQ, D_C), lambda b: (b, 2))],
        out_specs=pl.BlockSpec((SEQ, D_C), lambda b: (b, 0)),
        out_shape=jax.ShapeDtypeStruct((N_PROMPT, D_C), BF16),
        compiler_params=_cparams(1),
        name="ctx_attention",
    )(qkv, qkv, qkv)


ROWS_PER_STEP = 4
N_BIAS_CLASSES = 8


def _nattn_kernel(q_ref, k_ref, v_ref, ck_ref, cv_ref, bias_ref, o_ref):
    g = pl.program_id(1)
    n_loc = WIN_R * GRID_W
    for rr in range(ROWS_PER_STEP):
        r = g * ROWS_PER_STEP + rr
        r0 = jnp.clip(r - WIN_R // 2, 0, GRID_ROWS - WIN_R)
        cls = jnp.minimum(r, 4) + jnp.maximum(r - (GRID_ROWS - 4), 0)
        start = pl.multiple_of(r0 * GRID_W, GRID_W)
        rows = slice(rr * GRID_W, (rr + 1) * GRID_W)
        for j in range(NA_HEADS // 2):
            sl = slice(j * 128, (j + 1) * 128)
            qs = _pair_queries(q_ref[rows, sl])
            bias = jnp.concatenate([bias_ref[0, 2 * j, cls], bias_ref[0, 2 * j + 1, cls]], axis=0)
            s_loc = _dot_t(qs, k_ref[pl.ds(start, n_loc), sl]) + bias
            s_ctx = _dot_t(qs, ck_ref[0, :, sl])
            m = jnp.maximum(jnp.max(s_loc, axis=-1, keepdims=True), jnp.max(s_ctx, axis=-1, keepdims=True))
            e_loc = jnp.exp(s_loc - m)
            e_ctx = jnp.exp(s_ctx - m)
            den = jnp.sum(e_loc, axis=-1, keepdims=True) + jnp.sum(e_ctx, axis=-1, keepdims=True)
            pv = (jnp.dot(e_loc.astype(BF16), v_ref[pl.ds(start, n_loc), sl], preferred_element_type=F32)
                  + jnp.dot(e_ctx.astype(BF16), cv_ref[0, :, sl], preferred_element_type=F32))
            o_ref[rows, sl] = _pair_merge(pv / den, GRID_W).astype(BF16)


def _na_bias_table(rpb):
    r = np.array([0, 1, 2, 3, 4, GRID_ROWS - 3, GRID_ROWS - 2, GRID_ROWS - 1])
    r0 = np.clip(r - WIN_R // 2, 0, GRID_ROWS - WIN_R)
    dy = r0[:, None] + np.arange(WIN_R)[None] - r[:, None] + WIN_R - 1
    col = np.arange(GRID_W)
    c0 = np.clip(col - WIN_C // 2, 0, GRID_W - WIN_C)
    col_mask = (col[None] >= c0[:, None]) & (col[None] < c0[:, None] + WIN_C)
    dx = np.clip(col[None] - col[:, None], 1 - WIN_C, WIN_C - 1) + WIN_C - 1
    onehot = jnp.asarray(dx[None] == np.arange(2 * WIN_C - 1)[:, None, None], F32)
    by_dy = jnp.einsum('lhyx,xqk->lhyqk', rpb.astype(F32), onehot, precision=lax.Precision.HIGHEST)
    by_dy = jnp.where(jnp.asarray(col_mask), by_dy, NEG_BIG)
    per_class = [jnp.concatenate([by_dy[:, :, int(y)] for y in dy_c], axis=-1) for dy_c in dy]
    return jnp.stack(per_class, axis=2)


def _latent_attention(qkv, ctx_k, ctx_v, bias, l):
    n_groups = GRID_ROWS // ROWS_PER_STEP
    rows_q = ROWS_PER_STEP * GRID_W
    q_blocks_before = N_PROMPT // rows_q
    seq_blocks_before = N_PROMPT // DEC_SEQ
    return pl.pallas_call(
        _nattn_kernel,
        grid=(DEC_BATCH, n_groups),
        in_specs=[pl.BlockSpec((rows_q, D_C), lambda b, g: (q_blocks_before + b * n_groups + g, 0)),
                  pl.BlockSpec((DEC_SEQ, D_C), lambda b, g: (seq_blocks_before + b, 1)),
                  pl.BlockSpec((DEC_SEQ, D_C), lambda b, g: (seq_blocks_before + b, 2)),
                  pl.BlockSpec((1, PAST_LEN, D_C), lambda b, g: (b * DEPTH + l, 0, 0)),
                  pl.BlockSpec((1, PAST_LEN, D_C), lambda b, g: (b * DEPTH + l, 0, 0)),
                  pl.BlockSpec((1, NA_HEADS, N_BIAS_CLASSES, GRID_W, WIN_R * GRID_W),
                               lambda b, g: (l, 0, 0, 0, 0), pipeline_mode=pl.Buffered(1))],
        out_specs=pl.BlockSpec((rows_q, D_C), lambda b, g: (b * n_groups + g, 0)),
        out_shape=jax.ShapeDtypeStruct((DEC_BATCH * DEC_SEQ, D_C), BF16),
        compiler_params=_cparams(2),
        name="latent_attention",
    )(qkv, qkv, qkv, ctx_k, ctx_v, bias)


def _post_kernel(a_ref, b_ref, cp_ref, cs_ref, xp_ref, xs_ref, mod_ref, g2_ref, wo_ref, w1_ref, w2_ref, nf_ref,
                 op_ref, os_ref, cat, *, final):
    i = pl.program_id(0)
    is_prompt = i < N_PROMPT_TM
    cat[:, 0:D_A] = a_ref[...]
    cat[:, D_A:D_A + D_B] = b_ref[...]

    @pl.when(is_prompt)
    def _():
        cat[:, D_A + D_B:] = cp_ref[...]

    @pl.when(jnp.logical_not(is_prompt))
    def _():
        cat[:, D_A + D_B:] = cs_ref[...]

    m = mod_ref[0, 0]
    mix = jnp.dot(cat[...], wo_ref[0], preferred_element_type=F32)
    x1 = jnp.where(is_prompt, xp_ref[...], xs_ref[...]) + m[2:3] * mix
    h2 = (_rms(x1) * g2_ref[0] * (1.0 + m[4:5]) + m[3:4]).astype(BF16)
    acc = jnp.zeros((TM, D_MODEL), F32)
    for c in range(D_FF // D_MODEL):
        cols = slice(c * D_MODEL, (c + 1) * D_MODEL)
        u = jnp.dot(h2, w1_ref[0, :, cols], preferred_element_type=F32)
        u = jnp.square(jnp.maximum(u, 0.0)).astype(BF16)
        acc = acc + jnp.dot(u, w2_ref[0, cols, :], preferred_element_type=F32)
    x2 = x1 + m[5:6] * acc
    if final:
        x2 = _rms(x2) * nf_ref[...]

    @pl.when(is_prompt)
    def _():
        op_ref[...] = x2

    @pl.when(jnp.logical_not(is_prompt))
    def _():
        os_ref[...] = x2


def _post(out_a, out_b, out_cp, out_cs, xp, xs, mod, norm2, w_out, w1, w2, norm_f, l, final):
    const = lambda shape: pl.BlockSpec(shape, lambda i: (l,) + (0,) * (len(shape) - 1),
                                       pipeline_mode=pl.Buffered(1))
    prompt_rows = lambda width: pl.BlockSpec((TM, width), lambda i: (_prompt_tm_block(i), 0))
    latent_rows = lambda width: pl.BlockSpec((TM, width), lambda i: (_latent_tm_block(i), 0))
    return pl.pallas_call(
        functools.partial(_post_kernel, final=final),
        grid=(N_TOK // TM,),
        in_specs=[pl.BlockSpec((TM, D_A), lambda i: (i, 0)),
                  pl.BlockSpec((TM, D_B), lambda i: (i, 0)),
                  prompt_rows(D_C), latent_rows(D_C),
                  prompt_rows(D_MODEL), latent_rows(D_MODEL),
                  pl.BlockSpec((1, 1, 6, D_MODEL), lambda i: (l, _cond_of_tm_block(i), 0, 0)),
                  pl.BlockSpec((1, 1, D_MODEL), lambda i: (l, 0, 0)),
                  const((1, D_MODEL, D_MODEL)),
                  const((1, D_MODEL, D_FF)),
                  const((1, D_FF, D_MODEL)),
                  pl.BlockSpec((1, D_MODEL), lambda i: (0, 0))],
        out_specs=[prompt_rows(D_MODEL), latent_rows(D_MODEL)],
        out_shape=[jax.ShapeDtypeStruct((N_PROMPT, D_MODEL), F32),
                   jax.ShapeDtypeStruct((N_TOK - N_PROMPT, D_MODEL), F32)],
        scratch_shapes=[pltpu.VMEM((TM, D_MODEL), BF16)],
        compiler_params=_cparams(1),
        name="post_mlp",
    )(out_a, out_b, out_cp, out_cs, xp, xs, mod, norm2, w_out, w1, w2, norm_f)


def _block_diag(w):
    n, d, e = w.shape[-3:]
    eye = jnp.eye(n, dtype=w.dtype)
    full = w[..., :, :, None, :] * eye[:, None, :, None]
    return full.reshape(w.shape[:-3] + (n * d, n * e))


def kernel(x_prompt, x_sample, cache_k, cache_v, state_rglru, state_hgrn, c, c_ctx, w_mod, b_mod, norm1, norm2,
           w_in, rg_conv_w, rg_conv_b, rg_w_a, rg_b_a, rg_w_x, rg_b_x, rg_lambda, hg_lb, hg_norm, na_rpb,
           w_out, w1, w2, norm_f):
    cond = jnp.concatenate([c_ctx[None], c, jnp.zeros((N_COND - 1 - DEC_BATCH, D_MODEL), F32)], axis=0)
    w_in_b, w_out_b, w1_b, w2_b = (w.astype(BF16) for w in (w_in, w_out, w1, w2))
    wg = jnp.concatenate([_block_diag(rg_w_a[:, 0]), _block_diag(rg_w_x[:, 0]),
                          _block_diag(rg_w_a[:, 1]), _block_diag(rg_w_x[:, 1])], axis=-1).astype(BF16)
    bg = jnp.concatenate([rg_b_a[:, 0], rg_b_x[:, 0], rg_b_a[:, 1], rg_b_x[:, 1]], axis=-1)[:, None, :]
    lb_w = jax.nn.softmax(hg_lb.astype(F32), axis=0)
    hg_lower = jnp.cumsum(lb_w, axis=0) - lb_w[0]
    s0t = _block_diag(jnp.swapaxes(state_hgrn.astype(F32), -1, -2))
    ctx_k = cache_k.reshape(DEC_BATCH * DEPTH, PAST_LEN, D_C).astype(BF16)
    ctx_v = cache_v.reshape(DEC_BATCH * DEPTH, PAST_LEN, D_C).astype(BF16)
    bias = _na_bias_table(na_rpb)
    norm1_3, norm2_3 = norm1[:, None, :], norm2[:, None, :]
    conv_b3, hg_norm3 = rg_conv_b[:, None, :], hg_norm[:, None, :]
    norm_f2 = norm_f[None, :]

    mod = _modulation(cond, w_mod, b_mod)
    xp = x_prompt.reshape(N_PROMPT, D_MODEL)
    xs = x_sample.reshape(N_TOK - N_PROMPT, D_MODEL)

    new_k = jnp.zeros((BATCH, DEPTH, SEQ, D_C), F32)
    new_v = jnp.zeros((BATCH, DEPTH, SEQ, D_C), F32)
    new_hg = jnp.zeros((BATCH, DEPTH, 2, HG_HEADS, HG_DIM, HG_DIM), F32)
    new_rg = []
    for l in range(DEPTH):
        ab_proj, qkv, new_k, new_v = _in_proj(xp, xs, mod, norm1_3, w_in_b, new_k, new_v, l)
        out_a, fin_a = _rglru_mixer(ab_proj, state_rglru[:, l].astype(F32), rg_conv_w, conv_b3, wg, bg, rg_lambda, l)
        out_b, new_hg = _hgrn_mixer(ab_proj, hg_lower, hg_norm3, s0t[:, l], new_hg, l)
        out_cp = _ctx_attention(qkv)
        out_cs = _latent_attention(qkv, ctx_k, ctx_v, bias, l)
        xp, xs = _post(out_a, out_b, out_cp, out_cs, xp, xs, mod, norm2_3, w_out_b, w1_b, w2_b, norm_f2, l,
                       final=(l == DEPTH - 1))
        new_rg.append(fin_a[:BATCH])

    return (xp.reshape(BATCH, SEQ, D_MODEL), xs.reshape(DEC_BATCH, DEC_SEQ, D_MODEL),
            new_k.reshape(BATCH, DEPTH, SEQ, NA_HEADS, NA_DIM), new_v.reshape(BATCH, DEPTH, SEQ, NA_HEADS, NA_DIM),
            jnp.stack(new_rg, axis=1), new_hg)
```

```python
import functools

import numpy as np
import jax
import jax.numpy as jnp
from jax import lax
from jax.experimental import pallas as pl
from jax.experimental.pallas import tpu as pltpu

F32 = jnp.float32
BF16 = jnp.bfloat16

D_MODEL = 1024
BATCH = 32
SEQ = 256
DEPTH = 4
DEC_BATCH = 4
DEC_SEQ = 2048
PAST_LEN = 512
GRID_W = 64
GRID_ROWS = DEC_SEQ // GRID_W
D_A = 256
RG_BLOCKS = 4
RG_BLOCK_DIM = 64
RG_C = 8.0
D_B = 256
HG_HEADS = 4
HG_DIM = 64
HG_CHUNK = 16
D_C = 512
NA_DIM = 64
NA_HEADS = 8
NA_SCALE = NA_DIM ** -0.5
WIN_R = 8
WIN_C = 16
D_FF = 4 * D_MODEL
D_AB = 2 * D_A + 5 * D_B
D_IN = D_AB + 3 * D_C
EPS = 1e-6

N_PROMPT = BATCH * SEQ
N_TOK = N_PROMPT + DEC_BATCH * DEC_SEQ
TM = 512
N_PROMPT_TM = N_PROMPT // TM
MIX_ROWS = 2048
N_MIX_STEPS = N_TOK // MIX_ROWS
N_PROMPT_MIX = N_PROMPT // MIX_ROWS
SEG = 256
N_SEG = MIX_ROWS // SEG
N_COND = 8
NEG_BIG = -1e30

VMEM_LIMIT = 56 * 1024 * 1024


def _cparams(n_axes):
    return pltpu.CompilerParams(dimension_semantics=("arbitrary",) * n_axes,
                                vmem_limit_bytes=VMEM_LIMIT)


def _cond_of_tm_block(i):
    return jnp.where(i < N_PROMPT_TM, 0, 1 + (i - N_PROMPT_TM) // (DEC_SEQ // TM))


def _rms(x):
    return x * lax.rsqrt(jnp.mean(x * x, axis=-1, keepdims=True) + EPS)


def _mod_kernel(cond_ref, w_ref, b_ref, o_ref):
    s = jax.nn.silu(cond_ref[...])
    o_ref[0] = jnp.dot(s.astype(BF16), w_ref[0].astype(BF16), preferred_element_type=F32) + b_ref[0]


def _modulation(cond, w_mod, b_mod):
    nb = 6
    out = pl.pallas_call(
        _mod_kernel,
        grid=(DEPTH, nb),
        in_specs=[pl.BlockSpec((N_COND, D_MODEL), lambda l, j: (0, 0)),
                  pl.BlockSpec((1, D_MODEL, D_MODEL), lambda l, j: (l, 0, j)),
                  pl.BlockSpec((1, 1, D_MODEL), lambda l, j: (l, 0, j))],
        out_specs=pl.BlockSpec((1, N_COND, D_MODEL), lambda l, j: (l, 0, j)),
        out_shape=jax.ShapeDtypeStruct((DEPTH, N_COND, 6 * D_MODEL), F32),
        compiler_params=_cparams(2),
        name="modulation",
    )(cond, w_mod, b_mod.reshape(DEPTH, 1, 6 * D_MODEL))
    return out.reshape(DEPTH, N_COND, 6, D_MODEL)


def _prompt_tm_block(i):
    return jnp.minimum(i, N_PROMPT_TM - 1)


def _latent_tm_block(i):
    return jnp.clip(i - N_PROMPT_TM, 0, (N_TOK - N_PROMPT) // TM - 1)


def _in_kernel(xp_ref, xs_ref, mod_ref, g_ref, w_ref, kin_ref, vin_ref, ab_ref, qkv_ref, kc_ref, vc_ref):
    del kin_ref, vin_ref
    i = pl.program_id(0)
    m = mod_ref[0, 0]
    x = jnp.where(i < N_PROMPT_TM, xp_ref[...], xs_ref[...])
    h = _rms(x) * g_ref[0] * (1.0 + m[1:2]) + m[0:1]
    y = jnp.dot(h.astype(BF16), w_ref[0], preferred_element_type=F32)
    ab_ref[...] = y[:, :D_AB]
    qkv_ref[...] = y[:, D_AB:].astype(BF16)

    @pl.when(i < N_PROMPT_TM)
    def _():
        for j in range(TM // SEQ):
            rows = slice(j * SEQ, (j + 1) * SEQ)
            kc_ref[j, 0] = y[rows, D_AB + D_C:D_AB + 2 * D_C]
            vc_ref[j, 0] = y[rows, D_AB + 2 * D_C:]


def _in_proj(xp, xs, mod, norm1, w_in, k_buf, v_buf, l):
    cache_spec = pl.BlockSpec((TM // SEQ, 1, SEQ, D_C), lambda i: (_prompt_tm_block(i), l, 0, 0))
    cache_shape = jax.ShapeDtypeStruct((BATCH, DEPTH, SEQ, D_C), F32)
    return pl.pallas_call(
        _in_kernel,
        grid=(N_TOK // TM,),
        in_specs=[pl.BlockSpec((TM, D_MODEL), lambda i: (_prompt_tm_block(i), 0)),
                  pl.BlockSpec((TM, D_MODEL), lambda i: (_latent_tm_block(i), 0)),
                  pl.BlockSpec((1, 1, 6, D_MODEL), lambda i: (l, _cond_of_tm_block(i), 0, 0)),
                  pl.BlockSpec((1, 1, D_MODEL), lambda i: (l, 0, 0)),
                  pl.BlockSpec((1, D_MODEL, D_IN), lambda i: (l, 0, 0), pipeline_mode=pl.Buffered(1)),
                  pl.BlockSpec(memory_space=pl.ANY),
                  pl.BlockSpec(memory_space=pl.ANY)],
        out_specs=[pl.BlockSpec((TM, D_AB), lambda i: (i, 0)),
                   pl.BlockSpec((TM, 3 * D_C), lambda i: (i, 0)),
                   cache_spec, cache_spec],
        out_shape=[jax.ShapeDtypeStruct((N_TOK, D_AB), F32),
                   jax.ShapeDtypeStruct((N_TOK, 3 * D_C), BF16),
                   cache_shape, cache_shape],
        input_output_aliases={5: 2, 6: 3},
        compiler_params=_cparams(1),
        name="in_proj",
    )(xp, xs, mod, norm1, w_in, k_buf, v_buf)


def _scan_segment(a_ref, b_ref, slab, rb, carry, h_ref, hb, tile_ref, reverse):
    order = range(7, -1, -1) if reverse else range(8)
    ps, qs = {}, {}
    p = q = None
    for j in order:
        aj = a_ref[slab, pl.ds(rb + j, 32, stride=8), :]
        bj = b_ref[slab, pl.ds(rb + j, 32, stride=8), :]
        if p is None:
            p, q = aj, bj
        else:
            q = aj * q + bj
            p = aj * p
        ps[j], qs[j] = p, q
    tile_ref[0] = p
    tile_ref[1] = q
    for k in (range(31, -1, -1) if reverse else range(32)):
        tile_ref[2, k:k + 1, :] = carry
        carry = tile_ref[0, k:k + 1, :] * carry + tile_ref[1, k:k + 1, :]
    cin = tile_ref[2]
    for j in range(8):
        h_ref[slab, pl.ds(hb + j, 32, stride=8), :] = ps[j] * cin + qs[j]
    return carry


def _rglru_kernel(xa_ref, ga_ref, h0_ref, cw_ref, cb_ref, wg_ref, bg_ref, lam_ref, out_ref, fin_ref,
                  xseg, af, bf, ab, bb, hf, hseg, tiles):
    i = pl.program_id(0)
    is_prompt = i < N_PROMPT_MIX
    keep = jnp.where(is_prompt, 0.0, 1.0)
    lam = lam_ref[0]
    sp = jnp.maximum(-lam, 0.0) + jnp.log1p(jnp.exp(-jnp.abs(lam)))
    cw = cw_ref[0]
    cb = cb_ref[0]
    bg = bg_ref[0]
    h0 = h0_ref[0]

    def gates(xc, z, d):
        r = jax.nn.sigmoid(z[:, 2 * d * D_A:(2 * d + 1) * D_A])
        g = jax.nn.sigmoid(z[:, (2 * d + 1) * D_A:(2 * d + 2) * D_A])
        log_a = (-RG_C * r) * sp[d:d + 1]
        a = jnp.exp(log_a)
        return a, jnp.sqrt(-jnp.tanh(log_a) * (a * a + 1.0)) * (g * xc)

    def fwd_body(s, carry):
        c0, c1 = carry
        r0 = pl.multiple_of(s * SEG, SEG)
        prev_ok = jnp.where(s > 0, keep, 0.0)
        next_ok = jnp.where(s < N_SEG - 1, keep, 0.0)
        xseg[0:8, :] = xa_ref[pl.ds(pl.multiple_of(jnp.maximum(r0 - 8, 0), 8), 8), :] * prev_ok
        xseg[8:8 + SEG, :] = xa_ref[pl.ds(r0, SEG), :]
        xseg[8 + SEG:16 + SEG, :] = xa_ref[pl.ds(pl.multiple_of(jnp.minimum(r0 + SEG, MIX_ROWS - 8), 8), 8), :] * next_ok
        xc = cb
        for j in range(4):
            xc = xc + cw[j:j + 1] * xseg[6 + j:6 + j + SEG, :]
        z = jnp.dot(xc.astype(BF16), wg_ref[0], preferred_element_type=F32) + bg
        a_f, b_f = gates(xc, z, 0)
        a_b, b_b = gates(xc, z, 1)
        for c in range(2):
            sl = slice(c * 128, (c + 1) * 128)
            af[c] = a_f[:, sl]
            bf[c] = b_f[:, sl]
            ab[c, pl.ds(r0, SEG), :] = a_b[:, sl]
            bb[c, pl.ds(r0, SEG), :] = b_b[:, sl]
        first = s == 0
        c0 = jnp.where(first, h0[0:1, 0:128], c0) * keep
        c1 = jnp.where(first, h0[0:1, 128:256], c1) * keep
        c0 = _scan_segment(af, bf, 0, 0, c0, hf, r0, tiles.at[0], False)
        c1 = _scan_segment(af, bf, 1, 0, c1, hf, r0, tiles.at[1], False)
        fin_ref[s, 0:1, 0:128] = c0
        fin_ref[s, 0:1, 128:256] = c1
        return c0, c1

    zero = jnp.zeros((1, 128), F32)
    lax.fori_loop(0, N_SEG, fwd_body, (zero, zero))

    def bwd_body(t, carry):
        c0, c1 = carry
        s = N_SEG - 1 - t
        r0 = pl.multiple_of(s * SEG, SEG)
        first = t == 0
        c0 = jnp.where(first, h0[1:2, 0:128], c0) * keep
        c1 = jnp.where(first, h0[1:2, 128:256], c1) * keep
        c0 = _scan_segment(ab, bb, 0, r0, c0, hseg, 0, tiles.at[0], True)
        c1 = _scan_segment(ab, bb, 1, r0, c1, hseg, 0, tiles.at[1], True)
        fin_ref[s, 1:2, 0:128] = c0
        fin_ref[s, 1:2, 128:256] = c1
        for c in range(2):
            sl = slice(c * 128, (c + 1) * 128)
            ya = hf[c, pl.ds(r0, SEG), :] + hseg[c]
            out_ref[pl.ds(r0, SEG), sl] = (jax.nn.gelu(ga_ref[pl.ds(r0, SEG), sl]) * ya).astype(BF16)
        return c0, c1

    lax.fori_loop(0, N_SEG, bwd_body, (zero, zero))


def _rglru_mixer(ab_proj, h0, conv_w, conv_b, wg, bg, lam, l):
    latent = lambda i: jnp.maximum(i - N_PROMPT_MIX, 0)
    return pl.pallas_call(
        _rglru_kernel,
        grid=(N_MIX_STEPS,),
        in_specs=[pl.BlockSpec((MIX_ROWS, D_A), lambda i: (i, 0)),
                  pl.BlockSpec((MIX_ROWS, D_A), lambda i: (i, 1)),
                  pl.BlockSpec((1, 2, D_A), lambda i: (latent(i), 0, 0)),
                  pl.BlockSpec((1, 4, D_A), lambda i: (l, 0, 0)),
                  pl.BlockSpec((1, 1, D_A), lambda i: (l, 0, 0)),
                  pl.BlockSpec((1, D_A, 4 * D_A), lambda i: (l, 0, 0)),
                  pl.BlockSpec((1, 1, 4 * D_A), lambda i: (l, 0, 0)),
                  pl.BlockSpec((1, 2, D_A), lambda i: (l, 0, 0))],
        out_specs=[pl.BlockSpec((MIX_ROWS, D_A), lambda i: (i, 0)),
                   pl.BlockSpec((N_SEG, 2, D_A), lambda i: (i, 0, 0))],
        out_shape=[jax.ShapeDtypeStruct((N_TOK, D_A), BF16),
                   jax.ShapeDtypeStruct((N_MIX_STEPS * N_SEG, 2, D_A), F32)],
        scratch_shapes=[pltpu.VMEM((SEG + 16, D_A), F32),
                        pltpu.VMEM((2, SEG, 128), F32), pltpu.VMEM((2, SEG, 128), F32),
                        pltpu.VMEM((2, MIX_ROWS, 128), F32), pltpu.VMEM((2, MIX_ROWS, 128), F32),
                        pltpu.VMEM((2, MIX_ROWS, 128), F32), pltpu.VMEM((2, SEG, 128), F32),
                        pltpu.VMEM((2, 3, 32, 128), F32)],
        compiler_params=_cparams(1),
        name="rglru_mixer",
    )(ab_proj, ab_proj, h0, conv_w, conv_b, wg, bg, lam)


def _split3(x):
    hi = x.astype(BF16)
    r1 = x - hi.astype(F32)
    mid = r1.astype(BF16)
    lo = (r1 - mid.astype(F32)).astype(BF16)
    return hi, mid, lo


def _hgrn_kernel(bq0, bq1, bf0, bf1, bb0, bb1, bi0, bi1, bg0, bg1, lb_ref, gn_ref, s0_ref, tri_ref, e_ref,
                 fin_in_ref, out_ref, fin_ref, of_scr, o_nat, cum_scr):
    del fin_in_ref
    i = pl.program_id(0)
    keep = jnp.where(i < N_PROMPT_MIX, 0.0, 1.0)
    lb = lb_ref[0]
    ones_e = e_ref[...]
    lane = lax.broadcasted_iota(jnp.int32, (1, D_B), 1)
    head_mask = [jnp.where(lane // HG_DIM == g, 1.0, 0.0) for g in range(HG_HEADS)]
    rr = lax.broadcasted_iota(jnp.int32, (D_B, D_B), 0) // HG_DIM
    cc = lax.broadcasted_iota(jnp.int32, (D_B, D_B), 1) // HG_DIM
    bd_mask = jnp.where(rr == cc, 1.0, 0.0)
    n_chunks = SEG // HG_CHUNK
    qi = lax.broadcasted_iota(jnp.int32, (HG_HEADS * (SEG // 2), SEG // 2), 0) % (SEG // 2)
    kj = lax.broadcasted_iota(jnp.int32, (HG_HEADS * (SEG // 2), SEG // 2), 1)
    same_block = {h: jnp.where(qi // h == kj // h, 1.0, 0.0) for h in (16, 32, 64)}

    def natural(refs, r0):
        return jnp.concatenate([ref[pl.ds(r0, SEG), :] for ref in refs], axis=1)

    def by_phase(refs, r0):
        return jnp.concatenate(
            [jnp.concatenate([ref[pl.ds(r0 + tau, n_chunks, stride=HG_CHUNK), :] for ref in refs], axis=1)
             for tau in range(HG_CHUNK)], axis=0)

    def one_direction(r0, d, state):
        rev = d == 1
        gate_refs = (bb0, bb1) if rev else (bf0, bf1)
        lbd = lb[d:d + 1]

        qd = jax.nn.silu(by_phase((bq0, bq1), r0))
        fp = lbd + (1.0 - lbd) * jax.nn.sigmoid(by_phase(gate_refs, r0))
        kp = 1.0 - fp
        vp = by_phase((bi0, bi1), r0)
        p_blocks, offsets = [], []
        for dd in range(HG_CHUNK):
            m = (HG_CHUNK - dd) * n_chunks
            lo = dd * n_chunks
            if rev:
                if dd > 0:
                    qd = qd[:m] * fp[lo - n_chunks:lo - n_chunks + m]
                p_blocks.append((qd * kp[lo:]).astype(BF16))
            else:
                if dd > 0:
                    qd = qd[n_chunks:] * fp[n_chunks:n_chunks + m]
                p_blocks.append((qd * kp[:m]).astype(BF16))
            offsets.append(sum(blk.shape[0] for blk in p_blocks[:-1]))
        sums = jnp.dot(jnp.concatenate(p_blocks, axis=0), ones_e, preferred_element_type=F32)

        def group(x, g):
            return x[g * n_chunks:(g + 1) * n_chunks]

        for tau in range(HG_CHUNK):
            acc = None
            for dd in range(HG_CHUNK - tau if rev else tau + 1):
                src = tau + dd if rev else tau - dd
                row = offsets[dd] + (tau if rev else src) * n_chunks
                term = sums[row:row + n_chunks] * group(vp, src)
                acc = term if acc is None else acc + term
            for c in range(2):
                o_nat[c, pl.ds(tau, n_chunks, stride=HG_CHUNK), :] = acc[:, c * 128:(c + 1) * 128]
        o = jnp.concatenate([o_nat[0], o_nat[1]], axis=1)

        q = jax.nn.silu(natural((bq0, bq1), r0))
        v = natural((bi0, bi1), r0)
        f = lbd + (1.0 - lbd) * jax.nn.sigmoid(natural(gate_refs, r0))
        k = 1.0 - f
        logf = jnp.log(f)

        tri = tri_ref[d]
        parts = jnp.dot(tri, jnp.concatenate(_split3(logf), axis=1), preferred_element_type=F32)
        cum = parts[:, :D_B] + parts[:, D_B:2 * D_B] + parts[:, 2 * D_B:]
        cum_scr[...] = cum

        vb = v.astype(BF16)
        half = SEG // 2
        levels = []
        b = 2 * HG_CHUNK
        while b <= SEG:
            h = b // 2
            qs_l, ks_l, dq, dk = [], [], [], []
            for m0 in range(0, SEG, b):
                if rev:
                    qs, ks, ref_row = slice(m0, m0 + h), slice(m0 + h, m0 + b), m0 + h
                else:
                    qs, ks, ref_row = slice(m0 + h, m0 + b), slice(m0, m0 + h), m0 + h - 1
                ref = cum_scr[ref_row:ref_row + 1, :]
                qs_l.append(qs)
                ks_l.append(ks)
                dq.append(cum[qs] - ref)
                dk.append(ref - cum[ks])
            qt = jnp.concatenate([q[s] for s in qs_l], axis=0) * jnp.exp(jnp.concatenate(dq, axis=0))
            kt = (jnp.concatenate([k[s] for s in ks_l], axis=0) * jnp.exp(jnp.concatenate(dk, axis=0))).astype(BF16)
            qst = jnp.concatenate([qt * head_mask[g] for g in range(HG_HEADS)], axis=0).astype(BF16)
            levels.append((h, qst, kt, jnp.concatenate([vb[s] for s in ks_l], axis=0)))
            b *= 2
        scores = [lax.dot_general(qst, kt, (((1,), (1,)), ((), ())), preferred_element_type=F32)
                  for _, qst, kt, _ in levels]
        pvs = []
        for (h, _, _, vk), sc in zip(levels, scores):
            if h < half:
                sc = sc * same_block[h]
            pvs.append(jnp.dot(sc.astype(BF16), vk, preferred_element_type=F32))
        for (h, _, _, _), pv in zip(levels, pvs):
            ob = pv[0:half] * head_mask[0]
            for g in range(1, HG_HEADS):
                ob = ob + pv[g * half:(g + 1) * half] * head_mask[g]
            zeros = jnp.zeros((h, D_B), F32)
            pieces = []
            for j in range(half // h):
                blk = ob[j * h:(j + 1) * h]
                pieces += [blk, zeros] if rev else [zeros, blk]
            o = o + jnp.concatenate(pieces, axis=0)

        tot_row = 0 if rev else SEG - 1
        tot = cum_scr[tot_row:tot_row + 1, :]
        qe = (q * jnp.exp(cum)).astype(BF16)
        o = o + lax.dot_general(qe, state.astype(BF16), (((1,), (1,)), ((), ())), preferred_element_type=F32)
        ke = (k * jnp.exp(tot - cum)).astype(BF16)
        ds = lax.dot_general(vb, ke, (((0,), (0,)), ((), ())), preferred_element_type=F32)
        state = state * jnp.exp(tot) + bd_mask * ds
        return o, state

    def store_state(s, d, state):
        @pl.when(i < N_PROMPT_MIX)
        def _():
            st = state.T
            for g in range(HG_HEADS):
                sl = slice(g * HG_DIM, (g + 1) * HG_DIM)
                fin_ref[s, 0, d, g] = st[sl, sl]

    def fwd_body(s, state):
        r0 = pl.multiple_of(s * SEG, SEG)
        state = jnp.where(s == 0, s0_ref[0, 0], state) * keep
        o, state = one_direction(r0, 0, state)
        of_scr[pl.ds(r0, SEG), :] = o
        store_state(s, 0, state)
        return state

    lax.fori_loop(0, N_SEG, fwd_body, jnp.zeros((D_B, D_B), F32))

    def bwd_body(t, state):
        s = N_SEG - 1 - t
        r0 = pl.multiple_of(s * SEG, SEG)
        state = jnp.where(t == 0, s0_ref[0, 1], state) * keep
        o, state = one_direction(r0, 1, state)
        store_state(s, 1, state)
        o = o + of_scr[pl.ds(r0, SEG), :]
        sq = o * o
        hi = sq.astype(BF16)
        lo = (sq - hi.astype(F32)).astype(BF16)
        both = jnp.dot(jnp.concatenate([hi, lo], axis=0), ones_e, preferred_element_type=F32)
        ms = (both[:SEG] + both[SEG:]) * (1.0 / HG_DIM)
        y = o * lax.rsqrt(ms + EPS) * gn_ref[0]
        out_ref[pl.ds(r0, SEG), :] = (y * jax.nn.silu(natural((bg0, bg1), r0))).astype(BF16)
        return state

    lax.fori_loop(0, N_SEG, bwd_body, jnp.zeros((D_B, D_B), F32))


def _hgrn_constants():
    t = np.arange(SEG)
    tril = (t[:, None] >= t[None, :]).astype(np.float32)
    tri = np.stack([tril, tril.T])
    hd = np.arange(D_B) // HG_DIM
    ones_e = (hd[:, None] == hd[None, :]).astype(np.float32)
    return jnp.asarray(tri, BF16), jnp.asarray(ones_e, BF16)


def _hgrn_mixer(ab_proj, lower, hg_norm, s0t, fin_buf, l):
    tri, ones_e = _hgrn_constants()
    latent = lambda i: jnp.maximum(i - N_PROMPT_MIX, 0)
    first_slab = 2 * D_A // 128
    slabs = [pl.BlockSpec((MIX_ROWS, 128), lambda i, c=c: (i, c)) for c in range(first_slab, first_slab + 10)]
    return pl.pallas_call(
        _hgrn_kernel,
        grid=(N_MIX_STEPS,),
        in_specs=slabs + [
                  pl.BlockSpec((1, 2, D_B), lambda i: (l, 0, 0)),
                  pl.BlockSpec((1, 1, D_B), lambda i: (l, 0, 0)),
                  pl.BlockSpec((1, 2, D_B, D_B), lambda i: (latent(i), 0, 0, 0)),
                  pl.BlockSpec((2, SEG, SEG), lambda i: (0, 0, 0)),
                  pl.BlockSpec((D_B, D_B), lambda i: (0, 0)),
                  pl.BlockSpec(memory_space=pl.ANY)],
        out_specs=[pl.BlockSpec((MIX_ROWS, D_B), lambda i: (i, 0)),
                   pl.BlockSpec((N_SEG, 1, 2, HG_HEADS, HG_DIM, HG_DIM),
                                lambda i: (jnp.minimum(i, N_PROMPT_MIX - 1), l, 0, 0, 0, 0))],
        out_shape=[jax.ShapeDtypeStruct((N_TOK, D_B), BF16),
                   jax.ShapeDtypeStruct((BATCH, DEPTH, 2, HG_HEADS, HG_DIM, HG_DIM), F32)],
        input_output_aliases={15: 1},
        scratch_shapes=[pltpu.VMEM((MIX_ROWS, D_B), F32),
                        pltpu.VMEM((2, SEG, 128), F32), pltpu.VMEM((SEG, D_B), F32)],
        compiler_params=_cparams(1),
        name="hgrn_mixer",
    )(*([ab_proj] * 10), lower, hg_norm, s0t, tri, ones_e, fin_buf)


def _pair_queries(q2):
    lane = lax.broadcasted_iota(jnp.int32, (1, 128), 1)
    lo = jnp.where(lane < NA_DIM, NA_SCALE, 0.0)
    hi = jnp.where(lane < NA_DIM, 0.0, NA_SCALE)
    qf = q2.astype(F32)
    return jnp.concatenate([qf * lo, qf * hi], axis=0).astype(BF16)


def _pair_merge(pv, n):
    lane = lax.broadcasted_iota(jnp.int32, (1, 128), 1)
    return jnp.where(lane < NA_DIM, pv[0:n], pv[n:2 * n])


def _dot_t(a, b):
    return lax.dot_general(a, b, (((1,), (1,)), ((), ())), preferred_element_type=F32)


def _ctx_attn_kernel(q_ref, k_ref, v_ref, o_ref):
    for j in range(NA_HEADS // 2):
        sl = slice(j * 128, (j + 1) * 128)
        qs = _pair_queries(q_ref[:, sl])
        s = _dot_t(qs, k_ref[:, sl])
        e = jnp.exp(s - jnp.max(s, axis=-1, keepdims=True))
        pv = jnp.dot(e.astype(BF16), v_ref[:, sl], preferred_element_type=F32)
        pv = pv / jnp.sum(e, axis=-1, keepdims=True)
        o_ref[:, sl] = _pair_merge(pv, SEQ).astype(BF16)


def _ctx_attention(qkv):
    return pl.pallas_call(
        _ctx_attn_kernel,
        grid=(BATCH,),
        in_specs=[pl.BlockSpec((SEQ, D_C), lambda b: (b, 0)),
                  pl.BlockSpec((SEQ, D_C), lambda b: (b, 1)),
                  pl.BlockSpec((SEQ, D_C), lambda b: (b, 2))],
        out_specs=pl.BlockSpec((SEQ, D_C), lambda b: (b, 0)),
        out_shape=jax.ShapeDtypeStruct((N_PROMPT, D_C), BF16),
        compiler_params=_cparams(1),
        name="ctx_attention",
    )(qkv, qkv, qkv)


ROWS_PER_STEP = 4
N_BIAS_CLASSES = 8


def _nattn_kernel(q_ref, k_ref, v_ref, ck_ref, cv_ref, bias_ref, o_ref):
    g = pl.program_id(1)
    n_loc = WIN_R * GRID_W
    nq = 2 * GRID_W
    for j in range(NA_HEADS // 2):
        sl = slice(j * 128, (j + 1) * 128)
        qs_all = jnp.concatenate([_pair_queries(q_ref[rr * GRID_W:(rr + 1) * GRID_W, sl])
                                  for rr in range(ROWS_PER_STEP)], axis=0)
        s_ctx_all = _dot_t(qs_all, ck_ref[0, :, sl])
        e_locs, e_ctxs, dens, starts = [], [], [], []
        for rr in range(ROWS_PER_STEP):
            r = g * ROWS_PER_STEP + rr
            r0 = jnp.clip(r - WIN_R // 2, 0, GRID_ROWS - WIN_R)
            cls = jnp.minimum(r, 4) + jnp.maximum(r - (GRID_ROWS - 4), 0)
            start = pl.multiple_of(r0 * GRID_W, GRID_W)
            bias = jnp.concatenate([bias_ref[0, 2 * j, cls], bias_ref[0, 2 * j + 1, cls]], axis=0)
            s_loc = _dot_t(qs_all[rr * nq:(rr + 1) * nq], k_ref[pl.ds(start, n_loc), sl]) + bias
            s_ctx = s_ctx_all[rr * nq:(rr + 1) * nq]
            m = jnp.maximum(jnp.max(s_loc, axis=-1, keepdims=True), jnp.max(s_ctx, axis=-1, keepdims=True))
            e_loc = jnp.exp(s_loc - m)
            e_ctx = jnp.exp(s_ctx - m)
            dens.append(jnp.sum(e_loc, axis=-1, keepdims=True) + jnp.sum(e_ctx, axis=-1, keepdims=True))
            e_locs.append(e_loc.astype(BF16))
            e_ctxs.append(e_ctx.astype(BF16))
            starts.append(start)
        pv_ctx_all = jnp.dot(jnp.concatenate(e_ctxs, axis=0), cv_ref[0, :, sl], preferred_element_type=F32)
        for rr in range(ROWS_PER_STEP):
            pv = (jnp.dot(e_locs[rr], v_ref[pl.ds(starts[rr], n_loc), sl], preferred_element_type=F32)
                  + pv_ctx_all[rr * nq:(rr + 1) * nq])
            o_ref[rr * GRID_W:(rr + 1) * GRID_W, sl] = _pair_merge(pv / dens[rr], GRID_W).astype(BF16)


def _na_bias_table(rpb):
    r = np.array([0, 1, 2, 3, 4, GRID_ROWS - 3, GRID_ROWS - 2, GRID_ROWS - 1])
    r0 = np.clip(r - WIN_R // 2, 0, GRID_ROWS - WIN_R)
    dy = r0[:, None] + np.arange(WIN_R)[None] - r[:, None] + WIN_R - 1
    col = np.arange(GRID_W)
    c0 = np.clip(col - WIN_C // 2, 0, GRID_W - WIN_C)
    col_mask = (col[None] >= c0[:, None]) & (col[None] < c0[:, None] + WIN_C)
    dx = np.clip(col[None] - col[:, None], 1 - WIN_C, WIN_C - 1) + WIN_C - 1
    onehot = jnp.asarray(dx[None] == np.arange(2 * WIN_C - 1)[:, None, None], F32)
    by_dy = jnp.einsum('lhyx,xqk->lhyqk', rpb.astype(F32), onehot, precision=lax.Precision.HIGHEST)
    by_dy = jnp.where(jnp.asarray(col_mask), by_dy, NEG_BIG)
    per_class = [jnp.concatenate([by_dy[:, :, int(y)] for y in dy_c], axis=-1) for dy_c in dy]
    return jnp.stack(per_class, axis=2)


def _latent_attention(qkv, ctx_k, ctx_v, bias, l):
    n_groups = GRID_ROWS // ROWS_PER_STEP
    rows_q = ROWS_PER_STEP * GRID_W
    q_blocks_before = N_PROMPT // rows_q
    seq_blocks_before = N_PROMPT // DEC_SEQ
    return pl.pallas_call(
        _nattn_kernel,
        grid=(DEC_BATCH, n_groups),
        in_specs=[pl.BlockSpec((rows_q, D_C), lambda b, g: (q_blocks_before + b * n_groups + g, 0)),
                  pl.BlockSpec((DEC_SEQ, D_C), lambda b, g: (seq_blocks_before + b, 1)),
                  pl.BlockSpec((DEC_SEQ, D_C), lambda b, g: (seq_blocks_before + b, 2)),
                  pl.BlockSpec((1, PAST_LEN, D_C), lambda b, g: (b * DEPTH + l, 0, 0)),
                  pl.BlockSpec((1, PAST_LEN, D_C), lambda b, g: (b * DEPTH + l, 0, 0)),
                  pl.BlockSpec((1, NA_HEADS, N_BIAS_CLASSES, GRID_W, WIN_R * GRID_W),
                               lambda b, g: (l, 0, 0, 0, 0), pipeline_mode=pl.Buffered(1))],
        out_specs=pl.BlockSpec((rows_q, D_C), lambda b, g: (b * n_groups + g, 0)),
        out_shape=jax.ShapeDtypeStruct((DEC_BATCH * DEC_SEQ, D_C), BF16),
        compiler_params=_cparams(2),
        name="latent_attention",
    )(qkv, qkv, qkv, ctx_k, ctx_v, bias)


def _post_kernel(a_ref, b_ref, cp_ref, cs_ref, xp_ref, xs_ref, mod_ref, g2_ref, wo_ref, w1_ref, w2_ref, nf_ref,
                 op_ref, os_ref, cat, *, final):
    i = pl.program_id(0)
    is_prompt = i < N_PROMPT_TM
    cat[:, 0:D_A] = a_ref[...]
    cat[:, D_A:D_A + D_B] = b_ref[...]

    @pl.when(is_prompt)
    def _():
        cat[:, D_A + D_B:] = cp_ref[...]

    @pl.when(jnp.logical_not(is_prompt))
    def _():
        cat[:, D_A + D_B:] = cs_ref[...]

    m = mod_ref[0, 0]
    mix = jnp.dot(cat[...], wo_ref[0], preferred_element_type=F32)
    x1 = jnp.where(is_prompt, xp_ref[...], xs_ref[...]) + m[2:3] * mix
    h2 = (_rms(x1) * g2_ref[0] * (1.0 + m[4:5]) + m[3:4]).astype(BF16)
    acc = jnp.zeros((TM, D_MODEL), F32)
    for c in range(D_FF // D_MODEL):
        cols = slice(c * D_MODEL, (c + 1) * D_MODEL)
        u = jnp.dot(h2, w1_ref[0, :, cols], preferred_element_type=F32)
        u = jnp.square(jnp.maximum(u, 0.0)).astype(BF16)
        acc = acc + jnp.dot(u, w2_ref[0, cols, :], preferred_element_type=F32)
    x2 = x1 + m[5:6] * acc
    if final:
        x2 = _rms(x2) * nf_ref[...]

    @pl.when(is_prompt)
    def _():
        op_ref[...] = x2

    @pl.when(jnp.logical_not(is_prompt))
    def _():
        os_ref[...] = x2


def _post(out_a, out_b, out_cp, out_cs, xp, xs, mod, norm2, w_out, w1, w2, norm_f, l, final):
    const = lambda shape: pl.BlockSpec(shape, lambda i: (l,) + (0,) * (len(shape) - 1),
                                       pipeline_mode=pl.Buffered(1))
    prompt_rows = lambda width: pl.BlockSpec((TM, width), lambda i: (_prompt_tm_block(i), 0))
    latent_rows = lambda width: pl.BlockSpec((TM, width), lambda i: (_latent_tm_block(i), 0))
    return pl.pallas_call(
        functools.partial(_post_kernel, final=final),
        grid=(N_TOK // TM,),
        in_specs=[pl.BlockSpec((TM, D_A), lambda i: (i, 0)),
                  pl.BlockSpec((TM, D_B), lambda i: (i, 0)),
                  prompt_rows(D_C), latent_rows(D_C),
                  prompt_rows(D_MODEL), latent_rows(D_MODEL),
                  pl.BlockSpec((1, 1, 6, D_MODEL), lambda i: (l, _cond_of_tm_block(i), 0, 0)),
                  pl.BlockSpec((1, 1, D_MODEL), lambda i: (l, 0, 0)),
                  const((1, D_MODEL, D_MODEL)),
                  const((1, D_MODEL, D_FF)),
                  const((1, D_FF, D_MODEL)),
                  pl.BlockSpec((1, D_MODEL), lambda i: (0, 0))],
        out_specs=[prompt_rows(D_MODEL), latent_rows(D_MODEL)],
        out_shape=[jax.ShapeDtypeStruct((N_PROMPT, D_MODEL), F32),
                   jax.ShapeDtypeStruct((N_TOK - N_PROMPT, D_MODEL), F32)],
        scratch_shapes=[pltpu.VMEM((TM, D_MODEL), BF16)],
        compiler_params=_cparams(1),
        name="post_mlp",
    )(out_a, out_b, out_cp, out_cs, xp, xs, mod, norm2, w_out, w1, w2, norm_f)


def _block_diag(w):
    n, d, e = w.shape[-3:]
    eye = jnp.eye(n, dtype=w.dtype)
    full = w[..., :, :, None, :] * eye[:, None, :, None]
    return full.reshape(w.shape[:-3] + (n * d, n * e))


def kernel(x_prompt, x_sample, cache_k, cache_v, state_rglru, state_hgrn, c, c_ctx, w_mod, b_mod, norm1, norm2,
           w_in, rg_conv_w, rg_conv_b, rg_w_a, rg_b_a, rg_w_x, rg_b_x, rg_lambda, hg_lb, hg_norm, na_rpb,
           w_out, w1, w2, norm_f):
    cond = jnp.concatenate([c_ctx[None], c, jnp.zeros((N_COND - 1 - DEC_BATCH, D_MODEL), F32)], axis=0)
    w_in_b, w_out_b, w1_b, w2_b = (w.astype(BF16) for w in (w_in, w_out, w1, w2))
    wg = jnp.concatenate([_block_diag(rg_w_a[:, 0]), _block_diag(rg_w_x[:, 0]),
                          _block_diag(rg_w_a[:, 1]), _block_diag(rg_w_x[:, 1])], axis=-1).astype(BF16)
    bg = jnp.concatenate([rg_b_a[:, 0], rg_b_x[:, 0], rg_b_a[:, 1], rg_b_x[:, 1]], axis=-1)[:, None, :]
    lb_w = jax.nn.softmax(hg_lb.astype(F32), axis=0)
    hg_lower = jnp.cumsum(lb_w, axis=0) - lb_w[0]
    s0t = _block_diag(jnp.swapaxes(state_hgrn.astype(F32), -1, -2))
    ctx_k = cache_k.reshape(DEC_BATCH * DEPTH, PAST_LEN, D_C).astype(BF16)
    ctx_v = cache_v.reshape(DEC_BATCH * DEPTH, PAST_LEN, D_C).astype(BF16)
    bias = _na_bias_table(na_rpb)
    norm1_3, norm2_3 = norm1[:, None, :], norm2[:, None, :]
    conv_b3, hg_norm3 = rg_conv_b[:, None, :], hg_norm[:, None, :]
    norm_f2 = norm_f[None, :]

    mod = _modulation(cond, w_mod, b_mod)
    xp = x_prompt.reshape(N_PROMPT, D_MODEL)
    xs = x_sample.reshape(N_TOK - N_PROMPT, D_MODEL)

    new_k = jnp.zeros((BATCH, DEPTH, SEQ, D_C), F32)
    new_v = jnp.zeros((BATCH, DEPTH, SEQ, D_C), F32)
    new_hg = jnp.zeros((BATCH, DEPTH, 2, HG_HEADS, HG_DIM, HG_DIM), F32)
    new_rg = []
    for l in range(DEPTH):
        ab_proj, qkv, new_k, new_v = _in_proj(xp, xs, mod, norm1_3, w_in_b, new_k, new_v, l)
        out_a, fin_a = _rglru_mixer(ab_proj, state_rglru[:, l].astype(F32), rg_conv_w, conv_b3, wg, bg, rg_lambda, l)
        out_b, new_hg = _hgrn_mixer(ab_proj, hg_lower, hg_norm3, s0t[:, l], new_hg, l)
        out_cp = _ctx_attention(qkv)
        out_cs = _latent_attention(qkv, ctx_k, ctx_v, bias, l)
        xp, xs = _post(out_a, out_b, out_cp, out_cs, xp, xs, mod, norm2_3, w_out_b, w1_b, w2_b, norm_f2, l,
                       final=(l == DEPTH - 1))
        new_rg.append(fin_a[:BATCH])

    return (xp.reshape(BATCH, SEQ, D_MODEL), xs.reshape(DEC_BATCH, DEC_SEQ, D_MODEL),
            new_k.reshape(BATCH, DEPTH, SEQ, NA_HEADS, NA_DIM), new_v.reshape(BATCH, DEPTH, SEQ, NA_HEADS, NA_DIM),
            jnp.stack(new_rg, axis=1), new_hg)
```

```python
import functools

import numpy as np
import jax
import jax.numpy as jnp
from jax import lax
from jax.experimental import pallas as pl
from jax.experimental.pallas import tpu as pltpu

F32 = jnp.float32
BF16 = jnp.bfloat16

D_MODEL = 1024
BATCH = 32
SEQ = 256
DEPTH = 4
DEC_BATCH = 4
DEC_SEQ = 2048
PAST_LEN = 512
GRID_W = 64
GRID_ROWS = DEC_SEQ // GRID_W
D_A = 256
RG_BLOCKS = 4
RG_BLOCK_DIM = 64
RG_C = 8.0
D_B = 256
HG_HEADS = 4
HG_DIM = 64
HG_CHUNK = 16
D_C = 512
NA_DIM = 64
NA_HEADS = 8
NA_SCALE = NA_DIM ** -0.5
WIN_R = 8
WIN_C = 16
D_FF = 4 * D_MODEL
D_AB = 2 * D_A + 5 * D_B
D_IN = D_AB + 3 * D_C
EPS = 1e-6

N_PROMPT = BATCH * SEQ
N_TOK = N_PROMPT + DEC_BATCH * DEC_SEQ
TM = 512
N_PROMPT_TM = N_PROMPT // TM
MIX_ROWS = 2048
N_MIX_STEPS = N_TOK // MIX_ROWS
N_PROMPT_MIX = N_PROMPT // MIX_ROWS
SEG = 256
N_SEG = MIX_ROWS // SEG
N_COND = 8
NEG_BIG = -1e30

VMEM_LIMIT = 56 * 1024 * 1024


def _cparams(n_axes):
    return pltpu.CompilerParams(dimension_semantics=("arbitrary",) * n_axes,
                                vmem_limit_bytes=VMEM_LIMIT)


def _cond_of_tm_block(i):
    return jnp.where(i < N_PROMPT_TM, 0, 1 + (i - N_PROMPT_TM) // (DEC_SEQ // TM))


def _rms(x):
    return x * lax.rsqrt(jnp.mean(x * x, axis=-1, keepdims=True) + EPS)


def _mod_kernel(cond_ref, w_ref, b_ref, o_ref):
    s = jax.nn.silu(cond_ref[...])
    o_ref[0] = jnp.dot(s.astype(BF16), w_ref[0].astype(BF16), preferred_element_type=F32) + b_ref[0]


def _modulation(cond, w_mod, b_mod):
    nb = 6
    out = pl.pallas_call(
        _mod_kernel,
        grid=(DEPTH, nb),
        in_specs=[pl.BlockSpec((N_COND, D_MODEL), lambda l, j: (0, 0)),
                  pl.BlockSpec((1, D_MODEL, D_MODEL), lambda l, j: (l, 0, j)),
                  pl.BlockSpec((1, 1, D_MODEL), lambda l, j: (l, 0, j))],
        out_specs=pl.BlockSpec((1, N_COND, D_MODEL), lambda l, j: (l, 0, j)),
        out_shape=jax.ShapeDtypeStruct((DEPTH, N_COND, 6 * D_MODEL), F32),
        compiler_params=_cparams(2),
        name="modulation",
    )(cond, w_mod, b_mod.reshape(DEPTH, 1, 6 * D_MODEL))
    return out.reshape(DEPTH, N_COND, 6, D_MODEL)


def _prompt_tm_block(i):
    return jnp.minimum(i, N_PROMPT_TM - 1)


def _latent_tm_block(i):
    return jnp.clip(i - N_PROMPT_TM, 0, (N_TOK - N_PROMPT) // TM - 1)


def _in_kernel(xp_ref, xs_ref, mod_ref, g_ref, w_ref, kin_ref, vin_ref, ab_ref, qkv_ref, kc_ref, vc_ref):
    del kin_ref, vin_ref
    i = pl.program_id(0)
    m = mod_ref[0, 0]
    x = jnp.where(i < N_PROMPT_TM, xp_ref[...], xs_ref[...])
    h = _rms(x) * g_ref[0] * (1.0 + m[1:2]) + m[0:1]
    y = jnp.dot(h.astype(BF16), w_ref[0], preferred_element_type=F32)
    ab_ref[...] = y[:, :D_AB]
    qkv_ref[...] = y[:, D_AB:].astype(BF16)

    @pl.when(i < N_PROMPT_TM)
    def _():
        for j in range(TM // SEQ):
            rows = slice(j * SEQ, (j + 1) * SEQ)
            kc_ref[j, 0] = y[rows, D_AB + D_C:D_AB + 2 * D_C]
            vc_ref[j, 0] = y[rows, D_AB + 2 * D_C:]


def _in_proj(xp, xs, mod, norm1, w_in, k_buf, v_buf, l):
    cache_spec = pl.BlockSpec((TM // SEQ, 1, SEQ, D_C), lambda i: (_prompt_tm_block(i), l, 0, 0))
    cache_shape = jax.ShapeDtypeStruct((BATCH, DEPTH, SEQ, D_C), F32)
    return pl.pallas_call(
        _in_kernel,
        grid=(N_TOK // TM,),
        in_specs=[pl.BlockSpec((TM, D_MODEL), lambda i: (_prompt_tm_block(i), 0)),
                  pl.BlockSpec((TM, D_MODEL), lambda i: (_latent_tm_block(i), 0)),
                  pl.BlockSpec((1, 1, 6, D_MODEL), lambda i: (l, _cond_of_tm_block(i), 0, 0)),
                  pl.BlockSpec((1, 1, D_MODEL), lambda i: (l, 0, 0)),
                  pl.BlockSpec((1, D_MODEL, D_IN), lambda i: (l, 0, 0), pipeline_mode=pl.Buffered(1)),
                  pl.BlockSpec(memory_space=pl.ANY),
                  pl.BlockSpec(memory_space=pl.ANY)],
        out_specs=[pl.BlockSpec((TM, D_AB), lambda i: (i, 0)),
                   pl.BlockSpec((TM, 3 * D_C), lambda i: (i, 0)),
                   cache_spec, cache_spec],
        out_shape=[jax.ShapeDtypeStruct((N_TOK, D_AB), F32),
                   jax.ShapeDtypeStruct((N_TOK, 3 * D_C), BF16),
                   cache_shape, cache_shape],
        input_output_aliases={5: 2, 6: 3},
        compiler_params=_cparams(1),
        name="in_proj",
    )(xp, xs, mod, norm1, w_in, k_buf, v_buf)


def _scan_segment(a_ref, b_ref, slab, rb, carry, h_ref, hb, tile_ref, reverse):
    order = range(7, -1, -1) if reverse else range(8)
    ps, qs = {}, {}
    p = q = None
    for j in order:
        aj = a_ref[slab, pl.ds(rb + j, 32, stride=8), :]
        bj = b_ref[slab, pl.ds(rb + j, 32, stride=8), :]
        if p is None:
            p, q = aj, bj
        else:
            q = aj * q + bj
            p = aj * p
        ps[j], qs[j] = p, q
    tile_ref[0] = p
    tile_ref[1] = q
    for k in (range(31, -1, -1) if reverse else range(32)):
        tile_ref[2, k:k + 1, :] = carry
        carry = tile_ref[0, k:k + 1, :] * carry + tile_ref[1, k:k + 1, :]
    cin = tile_ref[2]
    for j in range(8):
        h_ref[slab, pl.ds(hb + j, 32, stride=8), :] = ps[j] * cin + qs[j]
    return carry


def _rglru_kernel(xa_ref, ga_ref, h0_ref, cw_ref, cb_ref, wg_ref, bg_ref, lam_ref, out_ref, fin_ref,
                  xseg, af, bf, ab, bb, hf, hseg, tiles):
    i = pl.program_id(0)
    is_prompt = i < N_PROMPT_MIX
    keep = jnp.where(is_prompt, 0.0, 1.0)
    lam = lam_ref[0]
    sp = jnp.maximum(-lam, 0.0) + jnp.log1p(jnp.exp(-jnp.abs(lam)))
    cw = cw_ref[0]
    cb = cb_ref[0]
    bg = bg_ref[0]
    h0 = h0_ref[0]

    def gates(xc, z, d):
        r = jax.nn.sigmoid(z[:, 2 * d * D_A:(2 * d + 1) * D_A])
        g = jax.nn.sigmoid(z[:, (2 * d + 1) * D_A:(2 * d + 2) * D_A])
        log_a = (-RG_C * r) * sp[d:d + 1]
        a = jnp.exp(log_a)
        return a, jnp.sqrt(-jnp.tanh(log_a) * (a * a + 1.0)) * (g * xc)

    def fwd_body(s, carry):
        c0, c1 = carry
        r0 = pl.multiple_of(s * SEG, SEG)
        prev_ok = jnp.where(s > 0, keep, 0.0)
        next_ok = jnp.where(s < N_SEG - 1, keep, 0.0)
        xseg[0:8, :] = xa_ref[pl.ds(pl.multiple_of(jnp.maximum(r0 - 8, 0), 8), 8), :] * prev_ok
        xseg[8:8 + SEG, :] = xa_ref[pl.ds(r0, SEG), :]
        xseg[8 + SEG:16 + SEG, :] = xa_ref[pl.ds(pl.multiple_of(jnp.minimum(r0 + SEG, MIX_ROWS - 8), 8), 8), :] * next_ok
        xc = cb
        for j in range(4):
            xc = xc + cw[j:j + 1] * xseg[6 + j:6 + j + SEG, :]
        z = jnp.dot(xc.astype(BF16), wg_ref[0], preferred_element_type=F32) + bg
        a_f, b_f = gates(xc, z, 0)
        a_b, b_b = gates(xc, z, 1)
        for c in range(2):
            sl = slice(c * 128, (c + 1) * 128)
            af[c] = a_f[:, sl]
            bf[c] = b_f[:, sl]
            ab[c, pl.ds(r0, SEG), :] = a_b[:, sl]
            bb[c, pl.ds(r0, SEG), :] = b_b[:, sl]
        first = s == 0
        c0 = jnp.where(first, h0[0:1, 0:128], c0) * keep
        c1 = jnp.where(first, h0[0:1, 128:256], c1) * keep
        c0 = _scan_segment(af, bf, 0, 0, c0, hf, r0, tiles.at[0], False)
        c1 = _scan_segment(af, bf, 1, 0, c1, hf, r0, tiles.at[1], False)
        fin_ref[s, 0:1, 0:128] = c0
        fin_ref[s, 0:1, 128:256] = c1
        return c0, c1

    zero = jnp.zeros((1, 128), F32)
    lax.fori_loop(0, N_SEG, fwd_body, (zero, zero))

    def bwd_body(t, carry):
        c0, c1 = carry
        s = N_SEG - 1 - t
        r0 = pl.multiple_of(s * SEG, SEG)
        first = t == 0
        c0 = jnp.where(first, h0[1:2, 0:128], c0) * keep
        c1 = jnp.where(first, h0[1:2, 128:256], c1) * keep
        c0 = _scan_segment(ab, bb, 0, r0, c0, hseg, 0, tiles.at[0], True)
        c1 = _scan_segment(ab, bb, 1, r0, c1, hseg, 0, tiles.at[1], True)
        fin_ref[s, 1:2, 0:128] = c0
        fin_ref[s, 1:2, 128:256] = c1
        for c in range(2):
            sl = slice(c * 128, (c + 1) * 128)
            ya = hf[c, pl.ds(r0, SEG), :] + hseg[c]
            out_ref[pl.ds(r0, SEG), sl] = (jax.nn.gelu(ga_ref[pl.ds(r0, SEG), sl]) * ya).astype(BF16)
        return c0, c1

    lax.fori_loop(0, N_SEG, bwd_body, (zero, zero))


def _rglru_mixer(ab_proj, h0, conv_w, conv_b, wg, bg, lam, l):
    latent = lambda i: jnp.maximum(i - N_PROMPT_MIX, 0)
    return pl.pallas_call(
        _rglru_kernel,
        grid=(N_MIX_STEPS,),
        in_specs=[pl.BlockSpec((MIX_ROWS, D_A), lambda i: (i, 0)),
                  pl.BlockSpec((MIX_ROWS, D_A), lambda i: (i, 1)),
                  pl.BlockSpec((1, 2, D_A), lambda i: (latent(i), 0, 0)),
                  pl.BlockSpec((1, 4, D_A), lambda i: (l, 0, 0)),
                  pl.BlockSpec((1, 1, D_A), lambda i: (l, 0, 0)),
                  pl.BlockSpec((1, D_A, 4 * D_A), lambda i: (l, 0, 0)),
                  pl.BlockSpec((1, 1, 4 * D_A), lambda i: (l, 0, 0)),
                  pl.BlockSpec((1, 2, D_A), lambda i: (l, 0, 0))],
        out_specs=[pl.BlockSpec((MIX_ROWS, D_A), lambda i: (i, 0)),
                   pl.BlockSpec((N_SEG, 2, D_A), lambda i: (i, 0, 0))],
        out_shape=[jax.ShapeDtypeStruct((N_TOK, D_A), BF16),
                   jax.ShapeDtypeStruct((N_MIX_STEPS * N_SEG, 2, D_A), F32)],
        scratch_shapes=[pltpu.VMEM((SEG + 16, D_A), F32),
                        pltpu.VMEM((2, SEG, 128), F32), pltpu.VMEM((2, SEG, 128), F32),
                        pltpu.VMEM((2, MIX_ROWS, 128), F32), pltpu.VMEM((2, MIX_ROWS, 128), F32),
                        pltpu.VMEM((2, MIX_ROWS, 128), F32), pltpu.VMEM((2, SEG, 128), F32),
                        pltpu.VMEM((2, 3, 32, 128), F32)],
        compiler_params=_cparams(1),
        name="rglru_mixer",
    )(ab_proj, ab_proj, h0, conv_w, conv_b, wg, bg, lam)


def _split3(x):
    hi = x.astype(BF16)
    r1 = x - hi.astype(F32)
    mid = r1.astype(BF16)
    lo = (r1 - mid.astype(F32)).astype(BF16)
    return hi, mid, lo


def _hgrn_kernel(bq0, bq1, bf0, bf1, bb0, bb1, bi0, bi1, bg0, bg1, lb_ref, gn_ref, s0_ref, tri_ref, e_ref,
                 fin_in_ref, out_ref, fin_ref, of_scr, o_nat, cum_scr):
    del fin_in_ref
    i = pl.program_id(0)
    keep = jnp.where(i < N_PROMPT_MIX, 0.0, 1.0)
    lb = lb_ref[0]
    ones_e = e_ref[...]
    lane = lax.broadcasted_iota(jnp.int32, (1, D_B), 1)
    head_mask = [jnp.where(lane // HG_DIM == g, 1.0, 0.0) for g in range(HG_HEADS)]
    rr = lax.broadcasted_iota(jnp.int32, (D_B, D_B), 0) // HG_DIM
    cc = lax.broadcasted_iota(jnp.int32, (D_B, D_B), 1) // HG_DIM
    bd_mask = jnp.where(rr == cc, 1.0, 0.0)
    n_chunks = SEG // HG_CHUNK
    qi = lax.broadcasted_iota(jnp.int32, (HG_HEADS * (SEG // 2), SEG // 2), 0) % (SEG // 2)
    kj = lax.broadcasted_iota(jnp.int32, (HG_HEADS * (SEG // 2), SEG // 2), 1)
    same_block = {h: jnp.where(qi // h == kj // h, 1.0, 0.0) for h in (16, 32, 64)}

    def natural(refs, r0):
        return jnp.concatenate([ref[pl.ds(r0, SEG), :] for ref in refs], axis=1)

    def by_phase(refs, r0):
        return jnp.concatenate(
            [jnp.concatenate([ref[pl.ds(r0 + tau, n_chunks, stride=HG_CHUNK), :] for ref in refs], axis=1)
             for tau in range(HG_CHUNK)], axis=0)

    def one_direction(r0, d, state):
        rev = d == 1
        gate_refs = (bb0, bb1) if rev else (bf0, bf1)
        lbd = lb[d:d + 1]

        qd = jax.nn.silu(by_phase((bq0, bq1), r0))
        fp = lbd + (1.0 - lbd) * jax.nn.sigmoid(by_phase(gate_refs, r0))
        kp = 1.0 - fp
        vp = by_phase((bi0, bi1), r0)
        p_blocks, offsets = [], []
        for dd in range(HG_CHUNK):
            m = (HG_CHUNK - dd) * n_chunks
            lo = dd * n_chunks
            if rev:
                if dd > 0:
                    qd = qd[:m] * fp[lo - n_chunks:lo - n_chunks + m]
                p_blocks.append((qd * kp[lo:]).astype(BF16))
            else:
                if dd > 0:
                    qd = qd[n_chunks:] * fp[n_chunks:n_chunks + m]
                p_blocks.append((qd * kp[:m]).astype(BF16))
            offsets.append(sum(blk.shape[0] for blk in p_blocks[:-1]))
        sums = jnp.dot(jnp.concatenate(p_blocks, axis=0), ones_e, preferred_element_type=F32)

        def group(x, g):
            return x[g * n_chunks:(g + 1) * n_chunks]

        for tau in range(HG_CHUNK):
            acc = None
            for dd in range(HG_CHUNK - tau if rev else tau + 1):
                src = tau + dd if rev else tau - dd
                row = offsets[dd] + (tau if rev else src) * n_chunks
                term = sums[row:row + n_chunks] * group(vp, src)
                acc = term if acc is None else acc + term
            for c in range(2):
                o_nat[c, pl.ds(tau, n_chunks, stride=HG_CHUNK), :] = acc[:, c * 128:(c + 1) * 128]
        o = jnp.concatenate([o_nat[0], o_nat[1]], axis=1)

        q = jax.nn.silu(natural((bq0, bq1), r0))
        v = natural((bi0, bi1), r0)
        f = lbd + (1.0 - lbd) * jax.nn.sigmoid(natural(gate_refs, r0))
        k = 1.0 - f
        logf = jnp.log(f)

        tri = tri_ref[d]
        parts = jnp.dot(tri, jnp.concatenate(_split3(logf), axis=1), preferred_element_type=F32)
        cum = parts[:, :D_B] + parts[:, D_B:2 * D_B] + parts[:, 2 * D_B:]
        cum_scr[...] = cum

        vb = v.astype(BF16)
        half = SEG // 2
        levels = []
        b = 2 * HG_CHUNK
        while b <= SEG:
            h = b // 2
            qs_l, ks_l, dq, dk = [], [], [], []
            for m0 in range(0, SEG, b):
                if rev:
                    qs, ks, ref_row = slice(m0, m0 + h), slice(m0 + h, m0 + b), m0 + h
                else:
                    qs, ks, ref_row = slice(m0 + h, m0 + b), slice(m0, m0 + h), m0 + h - 1
                ref = cum_scr[ref_row:ref_row + 1, :]
                qs_l.append(qs)
                ks_l.append(ks)
                dq.append(cum[qs] - ref)
                dk.append(ref - cum[ks])
            qt = jnp.concatenate([q[s] for s in qs_l], axis=0) * jnp.exp(jnp.concatenate(dq, axis=0))
            kt = (jnp.concatenate([k[s] for s in ks_l], axis=0) * jnp.exp(jnp.concatenate(dk, axis=0))).astype(BF16)
            qst = jnp.concatenate([qt * head_mask[g] for g in range(HG_HEADS)], axis=0).astype(BF16)
            levels.append((h, qst, kt, jnp.concatenate([vb[s] for s in ks_l], axis=0)))
            b *= 2
        scores = [lax.dot_general(qst, kt, (((1,), (1,)), ((), ())), preferred_element_type=F32)
                  for _, qst, kt, _ in levels]
        pvs = []
        for (h, _, _, vk), sc in zip(levels, scores):
            if h < half:
                sc = sc * same_block[h]
            pvs.append(jnp.dot(sc.astype(BF16), vk, preferred_element_type=F32))
        for (h, _, _, _), pv in zip(levels, pvs):
            ob = pv[0:half] * head_mask[0]
            for g in range(1, HG_HEADS):
                ob = ob + pv[g * half:(g + 1) * half] * head_mask[g]
            zeros = jnp.zeros((h, D_B), F32)
            pieces = []
            for j in range(half // h):
                blk = ob[j * h:(j + 1) * h]
                pieces += [blk, zeros] if rev else [zeros, blk]
            o = o + jnp.concatenate(pieces, axis=0)

        tot_row = 0 if rev else SEG - 1
        tot = cum_scr[tot_row:tot_row + 1, :]
        qe = (q * jnp.exp(cum)).astype(BF16)
        o = o + lax.dot_general(qe, state.astype(BF16), (((1,), (1,)), ((), ())), preferred_element_type=F32)
        ke = (k * jnp.exp(tot - cum)).astype(BF16)
        ds = lax.dot_general(vb, ke, (((0,), (0,)), ((), ())), preferred_element_type=F32)
        state = state * jnp.exp(tot) + bd_mask * ds
        return o, state

    def store_state(s, d, state):
        @pl.when(i < N_PROMPT_MIX)
        def _():
            st = state.T
            for g in range(HG_HEADS):
                sl = slice(g * HG_DIM, (g + 1) * HG_DIM)
                fin_ref[s, 0, d, g] = st[sl, sl]

    def fwd_body(s, state):
        r0 = pl.multiple_of(s * SEG, SEG)
        state = jnp.where(s == 0, s0_ref[0, 0], state) * keep
        o, state = one_direction(r0, 0, state)
        of_scr[pl.ds(r0, SEG), :] = o
        store_state(s, 0, state)
        return state

    lax.fori_loop(0, N_SEG, fwd_body, jnp.zeros((D_B, D_B), F32))

    def bwd_body(t, state):
        s = N_SEG - 1 - t
        r0 = pl.multiple_of(s * SEG, SEG)
        state = jnp.where(t == 0, s0_ref[0, 1], state) * keep
        o, state = one_direction(r0, 1, state)
        store_state(s, 1, state)
        o = o + of_scr[pl.ds(r0, SEG), :]
        sq = o * o
        hi = sq.astype(BF16)
        lo = (sq - hi.astype(F32)).astype(BF16)
        both = jnp.dot(jnp.concatenate([hi, lo], axis=0), ones_e, preferred_element_type=F32)
        ms = (both[:SEG] + both[SEG:]) * (1.0 / HG_DIM)
        y = o * lax.rsqrt(ms + EPS) * gn_ref[0]
        out_ref[pl.ds(r0, SEG), :] = (y * jax.nn.silu(natural((bg0, bg1), r0))).astype(BF16)
        return state

    lax.fori_loop(0, N_SEG, bwd_body, jnp.zeros((D_B, D_B), F32))


def _hgrn_constants():
    t = np.arange(SEG)
    tril = (t[:, None] >= t[None, :]).astype(np.float32)
    tri = np.stack([tril, tril.T])
    hd = np.arange(D_B) // HG_DIM
    ones_e = (hd[:, None] == hd[None, :]).astype(np.float32)
    return jnp.asarray(tri, BF16), jnp.asarray(ones_e, BF16)


def _hgrn_mixer(ab_proj, lower, hg_norm, s0t, fin_buf, l):
    tri, ones_e = _hgrn_constants()
    latent = lambda i: jnp.maximum(i - N_PROMPT_MIX, 0)
    first_slab = 2 * D_A // 128
    slabs = [pl.BlockSpec((MIX_ROWS, 128), lambda i, c=c: (i, c)) for c in range(first_slab, first_slab + 10)]
    return pl.pallas_call(
        _hgrn_kernel,
        grid=(N_MIX_STEPS,),
        in_specs=slabs + [
                  pl.BlockSpec((1, 2, D_B), lambda i: (l, 0, 0)),
                  pl.BlockSpec((1, 1, D_B), lambda i: (l, 0, 0)),
                  pl.BlockSpec((1, 2, D_B, D_B), lambda i: (latent(i), 0, 0, 0)),
                  pl.BlockSpec((2, SEG, SEG), lambda i: (0, 0, 0)),
                  pl.BlockSpec((D_B, D_B), lambda i: (0, 0)),
                  pl.BlockSpec(memory_space=pl.ANY)],
        out_specs=[pl.BlockSpec((MIX_ROWS, D_B), lambda i: (i, 0)),
                   pl.BlockSpec((N_SEG, 1, 2, HG_HEADS, HG_DIM, HG_DIM),
                                lambda i: (jnp.minimum(i, N_PROMPT_MIX - 1), l, 0, 0, 0, 0))],
        out_shape=[jax.ShapeDtypeStruct((N_TOK, D_B), BF16),
                   jax.ShapeDtypeStruct((BATCH, DEPTH, 2, HG_HEADS, HG_DIM, HG_DIM), F32)],
        input_output_aliases={15: 1},
        scratch_shapes=[pltpu.VMEM((MIX_ROWS, D_B), F32),
                        pltpu.VMEM((2, SEG, 128), F32), pltpu.VMEM((SEG, D_B), F32)],
        compiler_params=_cparams(1),
        name="hgrn_mixer",
    )(*([ab_proj] * 10), lower, hg_norm, s0t, tri, ones_e, fin_buf)


def _pair_queries(q2):
    lane = lax.broadcasted_iota(jnp.int32, (1, 128), 1)
    lo = jnp.where(lane < NA_DIM, NA_SCALE, 0.0)
    hi = jnp.where(lane < NA_DIM, 0.0, NA_SCALE)
    qf = q2.astype(F32)
    return jnp.concatenate([qf * lo, qf * hi], axis=0).astype(BF16)


def _pair_merge(pv, n):
    lane = lax.broadcasted_iota(jnp.int32, (1, 128), 1)
    return jnp.where(lane < NA_DIM, pv[0:n], pv[n:2 * n])


def _dot_t(a, b):
    return lax.dot_general(a, b, (((1,), (1,)), ((), ())), preferred_element_type=F32)


def _ctx_attn_kernel(q_ref, k_ref, v_ref, o_ref):
    for j in range(NA_HEADS // 2):
        sl = slice(j * 128, (j + 1) * 128)
        qs = _pair_queries(q_ref[:, sl])
        s = _dot_t(qs, k_ref[:, sl])
        e = jnp.exp(s - jnp.max(s, axis=-1, keepdims=True))
        pv = jnp.dot(e.astype(BF16), v_ref[:, sl], preferred_element_type=F32)
        pv = pv / jnp.sum(e, axis=-1, keepdims=True)
        o_ref[:, sl] = _pair_merge(pv, SEQ).astype(BF16)


def _ctx_attention(qkv):
    return pl.pallas_call(
        _ctx_attn_kernel,
        grid=(BATCH,),
        in_specs=[pl.BlockSpec((SEQ, D_C), lambda b: (b, 0)),
                  pl.BlockSpec((SEQ, D_C), lambda b: (b, 1)),
                  pl.BlockSpec((SEQ, D_C), lambda b: (b, 2))],
        out_specs=pl.BlockSpec((SEQ, D_C), lambda b: (b, 0)),
        out_shape=jax.ShapeDtypeStruct((N_PROMPT, D_C), BF16),
        compiler_params=_cparams(1),
        name="ctx_attention",
    )(qkv, qkv, qkv)


ROWS_PER_STEP = 8
N_BIAS_CLASSES = 8


def _nattn_kernel(q_ref, k_ref, v_ref, ck_ref, cv_ref, bias_ref, o_ref):
    g = pl.program_id(1)
    n_loc = WIN_R * GRID_W
    nq = 2 * GRID_W
    for j in range(NA_HEADS // 2):
        sl = slice(j * 128, (j + 1) * 128)
        qs_all = jnp.concatenate([_pair_queries(q_ref[rr * GRID_W:(rr + 1) * GRID_W, sl])
                                  for rr in range(ROWS_PER_STEP)], axis=0)
        s_ctx_all = _dot_t(qs_all, ck_ref[0, :, sl])
        e_locs, e_ctxs, dens, starts = [], [], [], []
        for rr in range(ROWS_PER_STEP):
            r = g * ROWS_PER_STEP + rr
            r0 = jnp.clip(r - WIN_R // 2, 0, GRID_ROWS - WIN_R)
            cls = jnp.minimum(r, 4) + jnp.maximum(r - (GRID_ROWS - 4), 0)
            start = pl.multiple_of(r0 * GRID_W, GRID_W)
            bias = jnp.concatenate([bias_ref[0, 2 * j, cls], bias_ref[0, 2 * j + 1, cls]], axis=0)
            s_loc = _dot_t(qs_all[rr * nq:(rr + 1) * nq], k_ref[pl.ds(start, n_loc), sl]) + bias
            s_ctx = s_ctx_all[rr * nq:(rr + 1) * nq]
            m = jnp.maximum(jnp.max(s_loc, axis=-1, keepdims=True), jnp.max(s_ctx, axis=-1, keepdims=True))
            e_loc = jnp.exp(s_loc - m)
            e_ctx = jnp.exp(s_ctx - m)
            dens.append(jnp.sum(e_loc, axis=-1, keepdims=True) + jnp.sum(e_ctx, axis=-1, keepdims=True))
            e_locs.append(e_loc.astype(BF16))
            e_ctxs.append(e_ctx.astype(BF16))
            starts.append(start)
        pv_ctx_all = jnp.dot(jnp.concatenate(e_ctxs, axis=0), cv_ref[0, :, sl], preferred_element_type=F32)
        for rr in range(ROWS_PER_STEP):
            pv = (jnp.dot(e_locs[rr], v_ref[pl.ds(starts[rr], n_loc), sl], preferred_element_type=F32)
                  + pv_ctx_all[rr * nq:(rr + 1) * nq])
            o_ref[rr * GRID_W:(rr + 1) * GRID_W, sl] = _pair_merge(pv / dens[rr], GRID_W).astype(BF16)


def _na_bias_table(rpb):
    r = np.array([0, 1, 2, 3, 4, GRID_ROWS - 3, GRID_ROWS - 2, GRID_ROWS - 1])
    r0 = np.clip(r - WIN_R // 2, 0, GRID_ROWS - WIN_R)
    dy = r0[:, None] + np.arange(WIN_R)[None] - r[:, None] + WIN_R - 1
    col = np.arange(GRID_W)
    c0 = np.clip(col - WIN_C // 2, 0, GRID_W - WIN_C)
    col_mask = (col[None] >= c0[:, None]) & (col[None] < c0[:, None] + WIN_C)
    dx = np.clip(col[None] - col[:, None], 1 - WIN_C, WIN_C - 1) + WIN_C - 1
    onehot = jnp.asarray(dx[None] == np.arange(2 * WIN_C - 1)[:, None, None], F32)
    by_dy = jnp.einsum('lhyx,xqk->lhyqk', rpb.astype(F32), onehot, precision=lax.Precision.HIGHEST)
    by_dy = jnp.where(jnp.asarray(col_mask), by_dy, NEG_BIG)
    per_class = [jnp.concatenate([by_dy[:, :, int(y)] for y in dy_c], axis=-1) for dy_c in dy]
    return jnp.stack(per_class, axis=2)


def _latent_attention(qkv, ctx_k, ctx_v, bias, l):
    n_groups = GRID_ROWS // ROWS_PER_STEP
    rows_q = ROWS_PER_STEP * GRID_W
    q_blocks_before = N_PROMPT // rows_q
    seq_blocks_before = N_PROMPT // DEC_SEQ
    return pl.pallas_call(
        _nattn_kernel,
        grid=(DEC_BATCH, n_groups),
        in_specs=[pl.BlockSpec((rows_q, D_C), lambda b, g: (q_blocks_before + b * n_groups + g, 0)),
                  pl.BlockSpec((DEC_SEQ, D_C), lambda b, g: (seq_blocks_before + b, 1)),
                  pl.BlockSpec((DEC_SEQ, D_C), lambda b, g: (seq_blocks_before + b, 2)),
                  pl.BlockSpec((1, PAST_LEN, D_C), lambda b, g: (b * DEPTH + l, 0, 0)),
                  pl.BlockSpec((1, PAST_LEN, D_C), lambda b, g: (b * DEPTH + l, 0, 0)),
                  pl.BlockSpec((1, NA_HEADS, N_BIAS_CLASSES, GRID_W, WIN_R * GRID_W),
                               lambda b, g: (l, 0, 0, 0, 0), pipeline_mode=pl.Buffered(1))],
        out_specs=pl.BlockSpec((rows_q, D_C), lambda b, g: (b * n_groups + g, 0)),
        out_shape=jax.ShapeDtypeStruct((DEC_BATCH * DEC_SEQ, D_C), BF16),
        compiler_params=_cparams(2),
        name="latent_attention",
    )(qkv, qkv, qkv, ctx_k, ctx_v, bias)


def _post_kernel(a_ref, b_ref, cp_ref, cs_ref, xp_ref, xs_ref, mod_ref, g2_ref, wo_ref, w1_ref, w2_ref, nf_ref,
                 op_ref, os_ref, cat, *, final):
    i = pl.program_id(0)
    is_prompt = i < N_PROMPT_TM
    cat[:, 0:D_A] = a_ref[...]
    cat[:, D_A:D_A + D_B] = b_ref[...]

    @pl.when(is_prompt)
    def _():
        cat[:, D_A + D_B:] = cp_ref[...]

    @pl.when(jnp.logical_not(is_prompt))
    def _():
        cat[:, D_A + D_B:] = cs_ref[...]

    m = mod_ref[0, 0]
    mix = jnp.dot(cat[...], wo_ref[0], preferred_element_type=F32)
    x1 = jnp.where(is_prompt, xp_ref[...], xs_ref[...]) + m[2:3] * mix
    h2 = (_rms(x1) * g2_ref[0] * (1.0 + m[4:5]) + m[3:4]).astype(BF16)
    acc = jnp.zeros((TM, D_MODEL), F32)
    for c in range(D_FF // D_MODEL):
        cols = slice(c * D_MODEL, (c + 1) * D_MODEL)
        u = jnp.dot(h2, w1_ref[0, :, cols], preferred_element_type=F32)
        u = jnp.square(jnp.maximum(u, 0.0)).astype(BF16)
        acc = acc + jnp.dot(u, w2_ref[0, cols, :], preferred_element_type=F32)
    x2 = x1 + m[5:6] * acc
    if final:
        x2 = _rms(x2) * nf_ref[...]

    @pl.when(is_prompt)
    def _():
        op_ref[...] = x2

    @pl.when(jnp.logical_not(is_prompt))
    def _():
        os_ref[...] = x2


def _post(out_a, out_b, out_cp, out_cs, xp, xs, mod, norm2, w_out, w1, w2, norm_f, l, final):
    const = lambda shape: pl.BlockSpec(shape, lambda i: (l,) + (0,) * (len(shape) - 1),
                                       pipeline_mode=pl.Buffered(1))
    prompt_rows = lambda width: pl.BlockSpec((TM, width), lambda i: (_prompt_tm_block(i), 0))
    latent_rows = lambda width: pl.BlockSpec((TM, width), lambda i: (_latent_tm_block(i), 0))
    return pl.pallas_call(
        functools.partial(_post_kernel, final=final),
        grid=(N_TOK // TM,),
        in_specs=[pl.BlockSpec((TM, D_A), lambda i: (i, 0)),
                  pl.BlockSpec((TM, D_B), lambda i: (i, 0)),
                  prompt_rows(D_C), latent_rows(D_C),
                  prompt_rows(D_MODEL), latent_rows(D_MODEL),
                  pl.BlockSpec((1, 1, 6, D_MODEL), lambda i: (l, _cond_of_tm_block(i), 0, 0)),
                  pl.BlockSpec((1, 1, D_MODEL), lambda i: (l, 0, 0)),
                  const((1, D_MODEL, D_MODEL)),
                  const((1, D_MODEL, D_FF)),
                  const((1, D_FF, D_MODEL)),
                  pl.BlockSpec((1, D_MODEL), lambda i: (0, 0))],
        out_specs=[prompt_rows(D_MODEL), latent_rows(D_MODEL)],
        out_shape=[jax.ShapeDtypeStruct((N_PROMPT, D_MODEL), F32),
                   jax.ShapeDtypeStruct((N_TOK - N_PROMPT, D_MODEL), F32)],
        scratch_shapes=[pltpu.VMEM((TM, D_MODEL), BF16)],
        compiler_params=_cparams(1),
        name="post_mlp",
    )(out_a, out_b, out_cp, out_cs, xp, xs, mod, norm2, w_out, w1, w2, norm_f)


def _block_diag(w):
    n, d, e = w.shape[-3:]
    eye = jnp.eye(n, dtype=w.dtype)
    full = w[..., :, :, None, :] * eye[:, None, :, None]
    return full.reshape(w.shape[:-3] + (n * d, n * e))


def kernel(x_prompt, x_sample, cache_k, cache_v, state_rglru, state_hgrn, c, c_ctx, w_mod, b_mod, norm1, norm2,
           w_in, rg_conv_w, rg_conv_b, rg_w_a, rg_b_a, rg_w_x, rg_b_x, rg_lambda, hg_lb, hg_norm, na_rpb,
           w_out, w1, w2, norm_f):
    cond = jnp.concatenate([c_ctx[None], c, jnp.zeros((N_COND - 1 - DEC_BATCH, D_MODEL), F32)], axis=0)
    w_in_b, w_out_b, w1_b, w2_b = (w.astype(BF16) for w in (w_in, w_out, w1, w2))
    wg = jnp.concatenate([_block_diag(rg_w_a[:, 0]), _block_diag(rg_w_x[:, 0]),
                          _block_diag(rg_w_a[:, 1]), _block_diag(rg_w_x[:, 1])], axis=-1).astype(BF16)
    bg = jnp.concatenate([rg_b_a[:, 0], rg_b_x[:, 0], rg_b_a[:, 1], rg_b_x[:, 1]], axis=-1)[:, None, :]
    lb_w = jax.nn.softmax(hg_lb.astype(F32), axis=0)
    hg_lower = jnp.cumsum(lb_w, axis=0) - lb_w[0]
    s0t = _block_diag(jnp.swapaxes(state_hgrn.astype(F32), -1, -2))
    ctx_k = cache_k.reshape(DEC_BATCH * DEPTH, PAST_LEN, D_C).astype(BF16)
    ctx_v = cache_v.reshape(DEC_BATCH * DEPTH, PAST_LEN, D_C).astype(BF16)
    bias = _na_bias_table(na_rpb)
    norm1_3, norm2_3 = norm1[:, None, :], norm2[:, None, :]
    conv_b3, hg_norm3 = rg_conv_b[:, None, :], hg_norm[:, None, :]
    norm_f2 = norm_f[None, :]

    mod = _modulation(cond, w_mod, b_mod)
    xp = x_prompt.reshape(N_PROMPT, D_MODEL)
    xs = x_sample.reshape(N_TOK - N_PROMPT, D_MODEL)

    new_k = jnp.zeros((BATCH, DEPTH, SEQ, D_C), F32)
    new_v = jnp.zeros((BATCH, DEPTH, SEQ, D_C), F32)
    new_hg = jnp.zeros((BATCH, DEPTH, 2, HG_HEADS, HG_DIM, HG_DIM), F32)
    new_rg = []
    for l in range(DEPTH):
        ab_proj, qkv, new_k, new_v = _in_proj(xp, xs, mod, norm1_3, w_in_b, new_k, new_v, l)
        out_a, fin_a = _rglru_mixer(ab_proj, state_rglru[:, l].astype(F32), rg_conv_w, conv_b3, wg, bg, rg_lambda, l)
        out_b, new_hg = _hgrn_mixer(ab_proj, hg_lower, hg_norm3, s0t[:, l], new_hg, l)
        out_cp = _ctx_attention(qkv)
        out_cs = _latent_attention(qkv, ctx_k, ctx_v, bias, l)
        xp, xs = _post(out_a, out_b, out_cp, out_cs, xp, xs, mod, norm2_3, w_out_b, w1_b, w2_b, norm_f2, l,
                       final=(l == DEPTH - 1))
        new_rg.append(fin_a[:BATCH])

    return (xp.reshape(BATCH, SEQ, D_MODEL), xs.reshape(DEC_BATCH, DEC_SEQ, D_MODEL),
            new_k.reshape(BATCH, DEPTH, SEQ, NA_HEADS, NA_DIM), new_v.reshape(BATCH, DEPTH, SEQ, NA_HEADS, NA_DIM),
            jnp.stack(new_rg, axis=1), new_hg)
```
